```python
import jax, jax.numpy as jnp
from jax import lax
import numpy as np

D_MODEL = 1024
BATCH = 4
SEQ = 8192
DEPTH = 2

GRID_W = 64
CTX_LEN = 256
N_EVEN = (DEPTH + 1) // 2
N_ODD = DEPTH // 2
LRU_W = D_MODEL // 2
LRU_HEADS = 8
LRU_HD = LRU_W // LRU_HEADS
LRU_C = 8.0
CONV_W = 4
CONV_LEFT = 2
FFT_W = D_MODEL - LRU_W
FFT_GROUPS = 8
FFT_GD = FFT_W // FFT_GROUPS
AB_IN = 2 * LRU_W + FFT_W
NA_HEADS = 16
NA_HD = D_MODEL // NA_HEADS
NA_KH = 8
NA_KW = 16
FFN_HIDDEN = -(-8 * D_MODEL // (3 * 256)) * 256
EPS = 1e-6

kernel_name = 'hybrid_rglru_fnet_natten_dit_block'


def _rmsnorm(x, g):
    xf = x.astype(jnp.float32)
    y = xf * lax.rsqrt(jnp.mean(xf * xf, axis=-1, keepdims=True) + EPS) * g.astype(jnp.float32)
    return y.astype(x.dtype)


def _modulate(x, g, shift, scale):
    return _rmsnorm(x, g) * (1 + scale) + shift


def _swiglu(h, wg, wu, wd):
    return (jax.nn.silu(h @ wg) * (h @ wu)) @ wd


def _dwconv(u, w, b):
    T = u.shape[1]
    up = jnp.pad(u, ((0, 0), (CONV_LEFT, CONV_W - 1 - CONV_LEFT), (0, 0)))
    out = b
    for k in range(CONV_W):
        out = out + up[:, k:k + T] * w[k]
    return out


def _rglru_coeffs(u, w_a, b_a, w_i, b_i, lam):
    B, T, C = u.shape
    uh = u.reshape(B, T, LRU_HEADS, LRU_HD)
    r = jax.nn.sigmoid(jnp.einsum('bthi,hij->bthj', uh, w_a).reshape(B, T, C) + b_a)
    i = jax.nn.sigmoid(jnp.einsum('bthi,hij->bthj', uh, w_i).reshape(B, T, C) + b_i)
    log_a = -LRU_C * r * jax.nn.softplus(-lam.astype(jnp.float32))
    a = jnp.exp(log_a)
    bterm = jnp.sqrt(-jnp.expm1(2.0 * log_a)) * (i * u)
    return a, bterm


def _affine_combine(l, r):
    return (l[0] * r[0], r[0] * l[1] + r[1])


def _linear_scan(a, b, h0, reverse):
    if reverse:
        a, b = jnp.flip(a, 1), jnp.flip(b, 1)
    A, Bc = lax.associative_scan(_affine_combine, (a, b), axis=1)
    h = Bc + A * h0[:, None, :]
    return jnp.flip(h, 1) if reverse else h


def _fourier(f):
    B, T, _ = f.shape
    fg = f.astype(jnp.float32).reshape(B, T, FFT_GROUPS, FFT_GD)
    out = jnp.fft.fft2(fg, axes=(1, 3), norm='ortho').real
    return out.reshape(B, T, FFT_W).astype(f.dtype)


def _ab_mixer(hx, hc, w_in, conv_w, conv_b, w_a, b_a, w_i, b_i, lam, w_out, need_ctx):
    zx = hx @ w_in
    zc = hc @ w_in
    ux, gx, fx = zx[..., :LRU_W], zx[..., LRU_W:2 * LRU_W], zx[..., 2 * LRU_W:]
    uc, gc, fc = zc[..., :LRU_W], zc[..., LRU_W:2 * LRU_W], zc[..., 2 * LRU_W:]
    ux = _dwconv(ux, conv_w, conv_b).astype(jnp.float32)
    uc = _dwconv(uc, conv_w, conv_b).astype(jnp.float32)
    rx = jnp.zeros_like(ux)
    hc_dirs = []
    for d in range(2):
        rev = d == 1
        ac, bc = _rglru_coeffs(uc, w_a[d], b_a[d], w_i[d], b_i[d], lam[d])
        hcs = _linear_scan(ac, bc, jnp.zeros_like(uc[:, 0]), rev)
        h_end = hcs[:, 0] if rev else hcs[:, -1]
        ax, bx = _rglru_coeffs(ux, w_a[d], b_a[d], w_i[d], b_i[d], lam[d])
        rx = rx + _linear_scan(ax, bx, h_end, rev)
        hc_dirs.append(hcs)
    yx = jnp.concatenate([rx.astype(hx.dtype) * jax.nn.gelu(gx), _fourier(fx)], axis=-1) @ w_out
    yc = None
    if need_ctx:
        rc = (hc_dirs[0] + hc_dirs[1]).astype(hc.dtype)
        yc = jnp.concatenate([rc * jax.nn.gelu(gc), _fourier(fc)], axis=-1) @ w_out
    return yx, yc


def _na_mixer(hx, hc, w_qkv, rpb, w_out, need_ctx):
    B, S, D = hx.shape
    L = hc.shape[1]
    R = S // GRID_W
    kh = min(NA_KH, R)
    scale = NA_HD ** -0.5
    qkv = (hx @ w_qkv).reshape(B, R, GRID_W, 3, NA_HEADS, NA_HD)
    q = qkv[:, :, :, 0] * scale
    k = qkv[:, :, :, 1]
    v = qkv[:, :, :, 2]
    qkv_c = (hc @ w_qkv).reshape(B, L, 3, NA_HEADS, NA_HD)
    qc, kc, vc = qkv_c[:, :, 0] * scale, qkv_c[:, :, 1], qkv_c[:, :, 2]
    row_start = jnp.clip(jnp.arange(R) - kh // 2, 0, R - kh)
    col_start = jnp.clip(jnp.arange(GRID_W) - NA_KW // 2, 0, GRID_W - NA_KW)
    col_idx = col_start[:, None] + jnp.arange(NA_KW)[None, :]
    dc = col_idx - jnp.arange(GRID_W)[:, None] + NA_KW - 1
    rpb_cols = rpb[:, :, dc]
    n_loc = kh * NA_KW

    def row_attend(args):
        r, q_r = args
        rs = row_start[r]
        k_r = lax.dynamic_slice_in_dim(k, rs, kh, axis=1)[:, :, col_idx]
        v_r = lax.dynamic_slice_in_dim(v, rs, kh, axis=1)[:, :, col_idx]
        dr = rs + jnp.arange(kh) - r + NA_KH - 1
        bias = jnp.transpose(rpb_cols[:, dr], (0, 2, 1, 3)).astype(jnp.float32)
        s_loc = jnp.einsum('bqhd,bkqjhd->bhqkj', q_r, k_r).astype(jnp.float32) + bias
        s_ctx = jnp.einsum('bqhd,blhd->bhql', q_r, kc).astype(jnp.float32)
        s = jnp.concatenate([s_loc.reshape(B, NA_HEADS, GRID_W, n_loc), s_ctx], axis=-1)
        p = jax.nn.softmax(s, axis=-1).astype(v.dtype)
        p_loc = p[..., :n_loc].reshape(B, NA_HEADS, GRID_W, kh, NA_KW)
        return (jnp.einsum('bhqkj,bkqjhd->bqhd', p_loc, v_r)
                + jnp.einsum('bhql,blhd->bqhd', p[..., n_loc:], vc))

    o = lax.map(row_attend, (jnp.arange(R), jnp.moveaxis(q, 1, 0)))
    yx = jnp.moveaxis(o, 0, 1).reshape(B, S, D) @ w_out
    yc = None
    if need_ctx:
        pc = jax.nn.softmax(jnp.einsum('bqhd,blhd->bhql', qc, kc).astype(jnp.float32), axis=-1)
        yc = jnp.einsum('bhql,blhd->bqhd', pc.astype(vc.dtype), vc).reshape(B, L, D) @ w_out
    return yx, yc


def _normal(k, shape, s):
    return jax.random.normal(k, shape, jnp.float32) * s


def setup_inputs(seed: int = 0) -> dict:
    key = jax.random.key(seed)
    ks = jax.random.split(key, 25)
    D, F = D_MODEL, FFN_HIDDEN
    u = jax.random.uniform(ks[20], (N_EVEN, 2, LRU_W), jnp.float32, minval=0.9, maxval=0.999)
    a0 = u ** (1.0 / LRU_C)
    return {
        'x': _normal(ks[0], (BATCH, SEQ, D), 1.0),
        'c': _normal(ks[1], (BATCH, D), 1.0),
        'ctx': _normal(ks[2], (BATCH, CTX_LEN, D), 1.0),
        'c_ctx': _normal(ks[3], (D,), 1.0),
        'w_mod': _normal(ks[4], (DEPTH, D, 6 * D), 0.5 * D ** -0.5),
        'b_mod': _normal(ks[5], (DEPTH, 6 * D), 0.02),
        'g_pre_mix': 1.0 + _normal(ks[6], (DEPTH, D), 0.02),
        'g_post_mix': 1.0 + _normal(ks[7], (DEPTH, D), 0.02),
        'g_pre_ffn': 1.0 + _normal(ks[8], (DEPTH, D), 0.02),
        'g_post_ffn': 1.0 + _normal(ks[9], (DEPTH, D), 0.02),
        'w_ffn_gate': _normal(ks[10], (DEPTH, D, F), D ** -0.5),
        'w_ffn_up': _normal(ks[11], (DEPTH, D, F), D ** -0.5),
        'w_ffn_down': _normal(ks[12], (DEPTH, F, D), F ** -0.5),
        'w_in_ab': _normal(ks[13], (N_EVEN, D, AB_IN), D ** -0.5),
        'conv_w': _normal(ks[14], (N_EVEN, CONV_W, LRU_W), CONV_W ** -0.5),
        'conv_b': _normal(ks[15], (N_EVEN, LRU_W), 0.02),
        'lru_w_a': _normal(ks[16], (N_EVEN, 2, LRU_HEADS, LRU_HD, LRU_HD), LRU_HD ** -0.5),
        'lru_b_a': _normal(ks[17], (N_EVEN, 2, LRU_W), 0.02),
        'lru_w_i': _normal(ks[18], (N_EVEN, 2, LRU_HEADS, LRU_HD, LRU_HD), LRU_HD ** -0.5),
        'lru_b_i': _normal(ks[19], (N_EVEN, 2, LRU_W), 0.02),
        'lru_lam': jnp.log(a0) - jnp.log1p(-a0),
        'w_out_ab': _normal(ks[21], (N_EVEN, LRU_W + FFT_W, D), (LRU_W + FFT_W) ** -0.5),
        'w_qkv_na': _normal(ks[22], (N_ODD, D, 3 * D), D ** -0.5),
        'rpb_na': _normal(ks[23], (N_ODD, NA_HEADS, 2 * NA_KH - 1, 2 * NA_KW - 1), 0.1),
        'w_out_na': _normal(ks[24], (N_ODD, D, D), D ** -0.5),
    }


def reference(x, c, ctx, c_ctx, w_mod, b_mod, g_pre_mix, g_post_mix, g_pre_ffn, g_post_ffn,
              w_ffn_gate, w_ffn_up, w_ffn_down, w_in_ab, conv_w, conv_b, lru_w_a, lru_b_a,
              lru_w_i, lru_b_i, lru_lam, w_out_ab, w_qkv_na, rpb_na, w_out_na):
    c_act = jax.nn.silu(c)
    cc_act = jax.nn.silu(c_ctx)
    for l in range(DEPTH):
        last = l == DEPTH - 1
        mx = (c_act @ w_mod[l] + b_mod[l])[:, None, :]
        mc = (cc_act @ w_mod[l] + b_mod[l])[None, None, :]
        sh1, sc1, gt1, sh2, sc2, gt2 = jnp.split(mx, 6, axis=-1)
        csh1, csc1, cgt1, csh2, csc2, cgt2 = jnp.split(mc, 6, axis=-1)
        hx = _modulate(x, g_pre_mix[l], sh1, sc1)
        hc = _modulate(ctx, g_pre_mix[l], csh1, csc1)
        if l % 2 == 0:
            e = l // 2
            yx, yc = _ab_mixer(hx, hc, w_in_ab[e], conv_w[e], conv_b[e], lru_w_a[e], lru_b_a[e],
                               lru_w_i[e], lru_b_i[e], lru_lam[e], w_out_ab[e], not last)
        else:
            o = l // 2
            yx, yc = _na_mixer(hx, hc, w_qkv_na[o], rpb_na[o], w_out_na[o], not last)
        x = x + gt1 * _rmsnorm(yx, g_post_mix[l])
        fx = _modulate(x, g_pre_ffn[l], sh2, sc2)
        x = x + gt2 * _rmsnorm(_swiglu(fx, w_ffn_gate[l], w_ffn_up[l], w_ffn_down[l]), g_post_ffn[l])
        if not last:
            ctx = ctx + cgt1 * _rmsnorm(yc, g_post_mix[l])
            fc = _modulate(ctx, g_pre_ffn[l], csh2, csc2)
            ctx = ctx + cgt2 * _rmsnorm(_swiglu(fc, w_ffn_gate[l], w_ffn_up[l], w_ffn_down[l]), g_post_ffn[l])
    return x
```

```python
import functools
import math

import numpy as np
import jax
import jax.numpy as jnp
from jax import lax
from jax.experimental import pallas as pl
from jax.experimental.pallas import tpu as pltpu

F32 = jnp.float32
BF16 = jnp.bfloat16

EPS = 1e-6
LRU_C = 8.0
CONV_LEFT = 2
FFT_GROUPS = 8
GRID_W = 64
NA_KH = 8
NA_KW = 16
DFT_RADIX = 8
NEG = -1e30

VMEM_LIMIT_BYTES = 56 * 1024 * 1024
ROW_TILE = 512
LRU_TILE = 512
LANES = 128
MXU_DIM = 256


def _cparams(*sem):
    return pltpu.CompilerParams(dimension_semantics=sem, vmem_limit_bytes=VMEM_LIMIT_BYTES)


def _const_spec(shape):
    nd = len(shape)
    return pl.BlockSpec(shape, lambda *_: (0,) * nd, pipeline_mode=pl.Buffered(1))


def _sigmoid(x):
    return 0.5 * (1.0 + jnp.tanh(0.5 * x))


def _silu(x):
    return x * _sigmoid(x)


def _gelu_tanh(x):
    c = math.sqrt(2.0 / math.pi)
    return 0.5 * x * (1.0 + jnp.tanh(c * (x + 0.044715 * (x * x * x))))


def _rms(x, g):
    return x * lax.rsqrt(jnp.mean(x * x, axis=-1, keepdims=True) + EPS) * g


def _mod_kernel(cond_ref, w_ref, b_ref, o_ref):
    act = _silu(cond_ref[...]).astype(BF16)
    o_ref[0] = jnp.dot(act, w_ref[0].astype(BF16), preferred_element_type=F32) + b_ref[0]


def _mod_vectors(cond, w_mod, b_mod):
    depth, d, n = w_mod.shape
    tn = n // 4
    return pl.pallas_call(
        _mod_kernel,
        grid=(depth, n // tn),
        in_specs=[pl.BlockSpec(cond.shape, lambda l, j: (0, 0)),
                  pl.BlockSpec((1, d, tn), lambda l, j: (l, 0, j)),
                  pl.BlockSpec((1, 1, tn), lambda l, j: (l, 0, j))],
        out_specs=pl.BlockSpec((1, cond.shape[0], tn), lambda l, j: (l, 0, j)),
        out_shape=jax.ShapeDtypeStruct((depth, cond.shape[0], n), F32),
        compiler_params=_cparams("parallel", "parallel"),
        name="mod",
    )(cond, w_mod, b_mod.reshape(depth, 1, n))


def _inproj_kernel(x_ref, g_ref, sh_ref, sc_ref, w_ref, *o_refs, splits):
    h = _rms(x_ref[...], g_ref[...]) * (1.0 + sc_ref[0]) + sh_ref[0]
    hb = h.astype(BF16)
    for o_ref, (lo, hi, scale) in zip(o_refs, splits):
        z = jnp.dot(hb, w_ref[:, lo:hi], preferred_element_type=F32)
        if scale != 1.0:
            z = z * scale
        o_ref[...] = z.astype(o_ref.dtype)


def _inproj(x2, g, sh, sc, w, splits, dtypes, rows_per_cond, cond_base):
    n, d = x2.shape
    tm = min(ROW_TILE, n)
    tiles_per_cond = rows_per_cond // tm
    cond_map = lambda i: (cond_base + i // tiles_per_cond, 0, 0)
    return pl.pallas_call(
        functools.partial(_inproj_kernel, splits=splits),
        grid=(n // tm,),
        in_specs=[pl.BlockSpec((tm, d), lambda i: (i, 0)),
                  _const_spec((1, d)),
                  pl.BlockSpec((1, 1, d), cond_map),
                  pl.BlockSpec((1, 1, d), cond_map),
                  _const_spec(w.shape)],
        out_specs=[pl.BlockSpec((tm, hi - lo), lambda i: (i, 0)) for lo, hi, _ in splits],
        out_shape=[jax.ShapeDtypeStruct((n, hi - lo), dt) for (lo, hi, _), dt in zip(splits, dtypes)],
        compiler_params=_cparams("parallel"),
        name="inproj",
    )(x2, g.reshape(1, d), sh, sc, w)


def _lru_kernel(*refs, reverse, combine, tt, nt):
    if combine:
        (up_ref, u_ref, un_ref, h0_ref, cw_ref, cb_ref, wa_ref, wi_ref, ba_ref, bi_ref, lam_ref,
         ro_ref, g_ref, out_ref, hl_ref, ext_s, a_s, b_s, h_s) = refs
    else:
        (up_ref, u_ref, un_ref, h0_ref, cw_ref, cb_ref, wa_ref, wi_ref, ba_ref, bi_ref, lam_ref,
         out_ref, hl_ref, ext_s, a_s, b_s, h_s) = refs
    step = pl.program_id(1)
    tile = nt - 1 - step if reverse else step
    c = u_ref.shape[-1]
    half = c // 2

    @pl.when(step == 0)
    def _():
        h_s[...] = h0_ref[0, 0:1, :]

    ext_s[0:8, :] = jnp.where(tile > 0, up_ref[0], 0.0)
    ext_s[8:8 + tt, :] = u_ref[0]
    ext_s[8 + tt:16 + tt, :] = jnp.where(tile < nt - 1, un_ref[0], 0.0)
    uc = cb_ref[...]
    for k in range(cw_ref.shape[0]):
        off = 8 - CONV_LEFT + k
        uc = uc + ext_s[off:off + tt, :] * cw_ref[k:k + 1, :]

    ub = uc.astype(BF16)
    lam = lam_ref[...]
    softplus_neg_lam = jnp.maximum(-lam, 0.0) + jnp.log1p(jnp.exp(-jnp.abs(lam)))
    for hb in range(2):
        sl = slice(hb * half, (hb + 1) * half)
        r = _sigmoid(jnp.dot(ub[:, sl], wa_ref[hb], preferred_element_type=F32) + ba_ref[:, sl])
        i = _sigmoid(jnp.dot(ub[:, sl], wi_ref[hb], preferred_element_type=F32) + bi_ref[:, sl])
        log_a = (-LRU_C) * r * softplus_neg_lam[:, sl]
        a = jnp.exp(log_a)
        a_s[:, sl] = a
        b_s[:, sl] = jnp.sqrt(-jnp.tanh(log_a) * (a * a + 1.0)) * (i * uc[:, sl])

    row = lax.broadcasted_iota(jnp.int32, (8, c), 0)
    ng = tt // 8

    def group(jj, h):
        j = ng - 1 - jj if reverse else jj
        off = pl.multiple_of(j * 8, 8)
        a = a_s[pl.ds(off, 8), :]
        b = b_s[pl.ds(off, 8), :]
        for k in (1, 2, 4):
            shift = 8 - k if reverse else k
            keep = (row < 8 - k) if reverse else (row >= k)
            a_sh = pltpu.roll(a, shift, 0)
            b_sh = pltpu.roll(b, shift, 0)
            b = jnp.where(keep, a * b_sh + b, b)
            a = jnp.where(keep, a * a_sh, a)
        hj = b + a * h
        b_s[pl.ds(off, 8), :] = hj
        return hj[0:1, :] if reverse else hj[7:8, :]

    h_last = lax.fori_loop(0, ng, group, h_s[...], unroll=4)
    h_s[...] = h_last
    hl_ref[0] = jnp.broadcast_to(h_last, hl_ref.shape[1:])
    if combine:
        out_ref[0] = ((ro_ref[0] + b_s[...]) * _gelu_tanh(g_ref[0])).astype(out_ref.dtype)
    else:
        out_ref[0] = b_s[...]


def _lru_direction(u, h0, cw, cb, wa_bd, wi_bd, ba, bi, lam, reverse, other=None, gate=None):
    bsz, t, c = u.shape
    tt = min(LRU_TILE, t)
    nt = t // tt
    g8 = tt // 8
    combine = other is not None
    tile = (lambda s: nt - 1 - s) if reverse else (lambda s: s)
    cur = pl.BlockSpec((1, tt, c), lambda b, s: (b, tile(s), 0))
    in_specs = [pl.BlockSpec((1, 8, c), lambda b, s: (b, jnp.maximum(tile(s) * g8 - 1, 0), 0)),
                cur,
                pl.BlockSpec((1, 8, c), lambda b, s: (b, jnp.minimum((tile(s) + 1) * g8, t // 8 - 1), 0)),
                pl.BlockSpec((1, 8, c), lambda b, s: (b, 0, 0)),
                _const_spec(cw.shape), _const_spec((1, c)),
                _const_spec(wa_bd.shape), _const_spec(wi_bd.shape),
                _const_spec((1, c)), _const_spec((1, c)), _const_spec((1, c))]
    args = [u, u, u, h0, cw, cb.reshape(1, c), wa_bd, wi_bd, ba.reshape(1, c), bi.reshape(1, c),
            lam.reshape(1, c)]
    if combine:
        in_specs += [cur, cur]
        args += [other, gate]
    return pl.pallas_call(
        functools.partial(_lru_kernel, reverse=reverse, combine=combine, tt=tt, nt=nt),
        grid=(bsz, nt),
        in_specs=in_specs,
        out_specs=[cur, pl.BlockSpec((1, 8, c), lambda b, s: (b, 0, 0))],
        out_shape=[jax.ShapeDtypeStruct((bsz, t, c), BF16 if combine else F32),
                   jax.ShapeDtypeStruct((bsz, 8, c), F32)],
        scratch_shapes=[pltpu.VMEM((tt + 16, c), F32), pltpu.VMEM((tt, c), F32),
                        pltpu.VMEM((tt, c), F32), pltpu.VMEM((1, c), F32)],
        compiler_params=_cparams("parallel", "arbitrary"),
        name="lru_bwd" if reverse else "lru_fwd",
    )(*args)


def _block_diag_halves(w):
    h, d, _ = w.shape
    per = h // 2
    eye = jnp.eye(per, dtype=w.dtype)
    wh = w.reshape(2, per, d, d)
    bd = jnp.einsum("pq,cpij->cpiqj", eye, wh).reshape(2, per * d, per * d)
    return bd.astype(BF16)


def _fourier_kernel(w8c_ref, w8s_ref, x_ref, cs_ref, tc_ref, ts_ref, cc_ref, sc_ref, out_ref,
                    are_s, aim_s, *, n2):
    n1 = pl.program_id(2)
    radix = are_s.shape[0]
    y = jnp.dot(cs_ref[...], x_ref[0].astype(BF16), preferred_element_type=F32)
    yc, ys = y[:n2], y[n2:]
    tc, ts = tc_ref[0], ts_ref[0]
    zr = yc * tc - ys * ts
    zi = -(yc * ts + ys * tc)

    @pl.when(n1 == 0)
    def _():
        for k1 in range(radix):
            are_s[k1] = zr
            aim_s[k1] = zi

    @pl.when(n1 > 0)
    def _():
        for k1 in range(radix):
            wc = w8c_ref[n1, k1]
            ws = w8s_ref[n1, k1]
            are_s[k1] += zr * wc + zi * ws
            aim_s[k1] += zi * wc - zr * ws

    @pl.when(n1 == radix - 1)
    def _():
        for k1 in range(radix):
            o = (jnp.dot(are_s[k1].astype(BF16), cc_ref[...], preferred_element_type=F32)
                 + jnp.dot(aim_s[k1].astype(BF16), sc_ref[...], preferred_element_type=F32))
            out_ref[0, k1 * n2:(k1 + 1) * n2, :] = o.astype(out_ref.dtype)


def _fourier(f, group_dim):
    bsz, t, c = f.shape
    radix = DFT_RADIX
    n2 = t // radix
    cw = MXU_DIM
    nh = c // cw
    k = np.arange(n2)
    ang = 2.0 * np.pi * ((k[:, None] * k[None, :]) % n2) / n2
    cs = np.concatenate([np.cos(ang), np.sin(ang)], 0) / math.sqrt(t)
    n1 = np.arange(radix)
    ang_t = 2.0 * np.pi * ((n1[:, None] * k[None, :]) % t) / t
    ang8 = 2.0 * np.pi * ((n1[:, None] * n1[None, :]) % radix) / radix
    j = np.arange(cw)
    same = (j[:, None] // group_dim) == (j[None, :] // group_dim)
    ang_c = 2.0 * np.pi * ((j[:, None] * j[None, :]) % group_dim) / group_dim
    ccm = np.where(same, np.cos(ang_c), 0.0) / math.sqrt(group_dim)
    scm = np.where(same, np.sin(ang_c), 0.0) / math.sqrt(group_dim)
    smem = pl.BlockSpec(memory_space=pltpu.SMEM)
    return pl.pallas_call(
        functools.partial(_fourier_kernel, n2=n2),
        grid=(bsz, nh, radix),
        in_specs=[smem, smem,
                  pl.BlockSpec((1, n2, cw), lambda b, h, i: (b, 0, i * nh + h)),
                  _const_spec((2 * n2, n2)),
                  pl.BlockSpec((1, n2, 1), lambda b, h, i: (i, 0, 0)),
                  pl.BlockSpec((1, n2, 1), lambda b, h, i: (i, 0, 0)),
                  _const_spec((cw, cw)), _const_spec((cw, cw))],
        out_specs=pl.BlockSpec((1, t, cw), lambda b, h, i: (b, 0, h)),
        out_shape=jax.ShapeDtypeStruct((bsz, t, c), BF16),
        scratch_shapes=[pltpu.VMEM((radix, n2, cw), F32), pltpu.VMEM((radix, n2, cw), F32)],
        compiler_params=_cparams("parallel", "parallel", "arbitrary"),
        name="fourier",
    )(jnp.asarray(np.cos(ang8), F32), jnp.asarray(np.sin(ang8), F32),
      f.reshape(bsz, n2, radix * c),
      jnp.asarray(cs, F32).astype(BF16),
      jnp.asarray(np.cos(ang_t), F32).reshape(radix, n2, 1),
      jnp.asarray(np.sin(ang_t), F32).reshape(radix, n2, 1),
      jnp.asarray(ccm, F32).astype(BF16), jnp.asarray(scm, F32).astype(BF16))


def _natten_kernel(q_ref, k_ref, v_ref, kc_ref, vc_ref, tz_ref, o_ref, bias_s, *, rows, kh):
    rb = pl.program_id(2)
    nrb = pl.num_programs(2)
    qr_n, w, lanes = q_ref.shape[1:]
    hd = lanes // 2
    kr_n = 2 * qr_n
    r0 = rb * qr_n
    kr0 = jnp.clip(r0 - kh // 2, 0, rows - kr_n)
    lane = lax.broadcasted_iota(jnp.int32, (1, lanes), 1)

    @pl.when((rb <= 1) | (rb == nrb - 1))
    def _():
        for qr in range(qr_n):
            r = r0 + qr
            rs = jnp.clip(r - kh // 2, 0, rows - kh)
            for m in range(kr_n // 2):
                kr = kr0 + 2 * m
                ae = kr - r + (NA_KH - 1) + NA_KH
                kr_lane = kr + lane // hd
                ok = (kr_lane >= rs) & (kr_lane < rs + kh)
                for hh in range(2):
                    bias_s[hh, qr * w:(qr + 1) * w, m * lanes:(m + 1) * lanes] = jnp.where(
                        ok, tz_ref[hh, ae], NEG)

    q = q_ref[0].reshape(qr_n * w, lanes)
    kb = k_ref[0, pl.ds(kr0, kr_n)].reshape(kr_n * w, lanes)
    vb = v_ref[0, pl.ds(kr0, kr_n)].reshape(kr_n * w, lanes)
    kc = kc_ref[0]
    vc = vc_ref[0]
    nt = (((1,), (1,)), ((), ()))
    out = None
    for hh in range(2):
        mine = (lane // hd == hh).astype(F32).astype(BF16)
        qm = q * mine
        s_loc = lax.dot_general(qm, kb, nt, preferred_element_type=F32) + bias_s[hh]
        s_ctx = lax.dot_general(qm, kc, nt, preferred_element_type=F32)
        mx = jnp.maximum(jnp.max(s_loc, axis=-1, keepdims=True), jnp.max(s_ctx, axis=-1, keepdims=True))
        p_loc = jnp.exp(s_loc - mx)
        p_ctx = jnp.exp(s_ctx - mx)
        den = jnp.sum(p_loc, axis=-1, keepdims=True) + jnp.sum(p_ctx, axis=-1, keepdims=True)
        acc = (jnp.dot(p_loc.astype(BF16), vb * mine, preferred_element_type=F32)
               + jnp.dot(p_ctx.astype(BF16), vc * mine, preferred_element_type=F32))
        acc = acc / den
        out = acc if out is None else out + acc
    o_ref[0] = out.reshape(qr_n, w, lanes).astype(o_ref.dtype)


def _natten_bias_table(rpb):
    h = rpb.shape[0]
    w = GRID_W
    qc = np.arange(w)
    cs = np.clip(qc - NA_KW // 2, 0, w - NA_KW)
    kc = np.arange(w)
    inside = (kc[None, :] >= cs[:, None]) & (kc[None, :] < cs[:, None] + NA_KW)
    dc = np.clip(kc[None, :] - qc[:, None] + NA_KW - 1, 0, 2 * NA_KW - 2)
    slabs = jnp.where(inside[None, None], rpb[:, :, dc], NEG)
    n_slab = 4 * NA_KH + 1
    pad_lo = NA_KH
    pad_hi = n_slab - pad_lo - slabs.shape[1]
    slabs = jnp.pad(slabs, ((0, 0), (pad_lo, pad_hi), (0, 0), (0, 0)), constant_values=NEG)
    return jnp.concatenate([slabs[:, :-1], slabs[:, 1:]], axis=-1)


def _natten(q, k, v, kc, vc, table):
    bsz, rows, w, d = q.shape
    l = kc.shape[1]
    qr_n = NA_KH
    assert w == GRID_W and 2 * w == LANES and rows >= 2 * qr_n and rows % qr_n == 0
    kh = min(NA_KH, rows)
    nhp = d // LANES
    whole = pl.BlockSpec((1, rows, w, LANES), lambda b, hp, rb: (b, 0, 0, hp))
    ctx = pl.BlockSpec((1, l, LANES), lambda b, hp, rb: (b, 0, hp))
    blk = pl.BlockSpec((1, qr_n, w, LANES), lambda b, hp, rb: (b, rb, 0, hp))
    return pl.pallas_call(
        functools.partial(_natten_kernel, rows=rows, kh=kh),
        grid=(bsz, nhp, rows // qr_n),
        in_specs=[blk, whole, whole, ctx, ctx,
                  pl.BlockSpec((2,) + table.shape[1:], lambda b, hp, rb: (hp, 0, 0, 0))],
        out_specs=blk,
        out_shape=jax.ShapeDtypeStruct(q.shape, BF16),
        scratch_shapes=[pltpu.VMEM((2, qr_n * w, 2 * qr_n * w), F32)],
        compiler_params=_cparams("parallel", "parallel", "arbitrary"),
        name="natten",
    )(q, k, v, kc, vc, table)


def _tail_kernel(*refs, n_lhs, f_chunk):
    lhs = refs[:n_lhs]
    wos = refs[n_lhs:2 * n_lhs]
    (x_ref, gpm_ref, gt1_ref, gpf_ref, sh2_ref, sc2_ref, wg_ref, wu_ref, wd_ref, gpo_ref, gt2_ref,
     o_ref) = refs[2 * n_lhs:]
    y = None
    for a_ref, w_ref in zip(lhs, wos):
        part = jnp.dot(a_ref[...], w_ref[...], preferred_element_type=F32)
        y = part if y is None else y + part
    x1 = x_ref[...] + gt1_ref[0] * _rms(y, gpm_ref[...])
    fx = (_rms(x1, gpf_ref[...]) * (1.0 + sc2_ref[0]) + sh2_ref[0]).astype(BF16)
    ffn = None
    for lo in range(0, wg_ref.shape[1], f_chunk):
        gate = jnp.dot(fx, wg_ref[:, lo:lo + f_chunk], preferred_element_type=F32)
        up = jnp.dot(fx, wu_ref[:, lo:lo + f_chunk], preferred_element_type=F32)
        hid = (_silu(gate) * up).astype(BF16)
        part = jnp.dot(hid, wd_ref[lo:lo + f_chunk, :], preferred_element_type=F32)
        ffn = part if ffn is None else ffn + part
    o_ref[...] = x1 + gt2_ref[0] * _rms(ffn, gpo_ref[...])


def _tail(lhs, wos, x2, gpm, gt1, gpf, sh2, sc2, wg, wu, wd, gpo, gt2, rows_per_cond, cond_base):
    n, d = x2.shape
    f = wg.shape[1]
    tm = min(ROW_TILE, n)
    tiles_per_cond = rows_per_cond // tm
    cond_map = lambda i: (cond_base + i // tiles_per_cond, 0, 0)
    cond = pl.BlockSpec((1, 1, d), cond_map)
    vec = _const_spec((1, d))
    return pl.pallas_call(
        functools.partial(_tail_kernel, n_lhs=len(lhs), f_chunk=f // 2),
        grid=(n // tm,),
        in_specs=([pl.BlockSpec((tm, a.shape[1]), lambda i: (i, 0)) for a in lhs]
                  + [_const_spec(w.shape) for w in wos]
                  + [pl.BlockSpec((tm, d), lambda i: (i, 0)), vec, cond, vec, cond, cond,
                     _const_spec(wg.shape), _const_spec(wu.shape), _const_spec(wd.shape), vec, cond]),
        out_specs=pl.BlockSpec((tm, d), lambda i: (i, 0)),
        out_shape=jax.ShapeDtypeStruct((n, d), F32),
        compiler_params=_cparams("parallel"),
        name="tail",
    )(*lhs, *wos, x2, gpm.reshape(1, d), gt1, gpf.reshape(1, d), sh2, sc2, wg, wu, wd,
      gpo.reshape(1, d), gt2)


def kernel(x, c, ctx, c_ctx, w_mod, b_mod, g_pre_mix, g_post_mix, g_pre_ffn, g_post_ffn, w_ffn_gate, w_ffn_up, w_ffn_down, w_in_ab, conv_w, conv_b, lru_w_a, lru_b_a, lru_w_i, lru_b_i, lru_lam, w_out_ab, w_qkv_na, rpb_na, w_out_na):
    bsz, seq, d = x.shape
    ctx_len = ctx.shape[1]
    depth = w_mod.shape[0]
    lru_w = conv_w.shape[-1]
    n_cond = 8
    assert bsz < n_cond
    ctx_cond = bsz

    cond = jnp.concatenate([c, c_ctx[None], jnp.zeros((n_cond - bsz - 1, d), F32)], 0)
    mod = _mod_vectors(cond, w_mod, b_mod).reshape(depth, n_cond, 6, 1, d)

    xs = x.reshape(bsz * seq, d)
    cs = ctx.reshape(bsz * ctx_len, d)
    for l in range(depth):
        last = l == depth - 1
        sh1, sc1, gt1, sh2, sc2, gt2 = (mod[l, :, j] for j in range(6))
        wg, wu, wd = (w[l].astype(BF16) for w in (w_ffn_gate, w_ffn_up, w_ffn_down))
        norm_args = (g_post_mix[l], gt1, g_pre_ffn[l], sh2, sc2, wg, wu, wd, g_post_ffn[l], gt2)
        if l % 2 == 0:
            e = l // 2
            w_in = w_in_ab[e].astype(BF16)
            w_out = w_out_ab[e].astype(BF16)
            splits = ((0, lru_w, 1.0), (lru_w, 2 * lru_w, 1.0), (2 * lru_w, w_in.shape[1], 1.0))
            gates = [(_block_diag_halves(lru_w_a[e, dr]), _block_diag_halves(lru_w_i[e, dr]))
                     for dr in range(2)]

            def mixer(tokens, t_len, rows_per_cond, cond_base, h0):
                u, g, f = _inproj(tokens, g_pre_mix[l], sh1, sc1, w_in, splits, (F32, F32, F32),
                                  rows_per_cond, cond_base)
                u, g, f = (a.reshape(bsz, t_len, -1) for a in (u, g, f))
                lru = functools.partial(_lru_direction, u, cw=conv_w[e], cb=conv_b[e])
                hf, end_f = lru(h0[0], wa_bd=gates[0][0], wi_bd=gates[0][1], ba=lru_b_a[e, 0],
                                bi=lru_b_i[e, 0], lam=lru_lam[e, 0], reverse=False)
                lr, end_b = lru(h0[1], wa_bd=gates[1][0], wi_bd=gates[1][1], ba=lru_b_a[e, 1],
                                bi=lru_b_i[e, 1], lam=lru_lam[e, 1], reverse=True, other=hf, gate=g)
                fo = _fourier(f, f.shape[-1] // FFT_GROUPS)
                lhs = [lr.reshape(tokens.shape[0], -1), fo.reshape(tokens.shape[0], -1)]
                return lhs, (end_f, end_b)

            zeros = jnp.zeros((bsz, 8, lru_w), F32)
            lhs_c, ends = mixer(cs, ctx_len, bsz * ctx_len, ctx_cond, (zeros, zeros))
            lhs_x, _ = mixer(xs, seq, seq, 0, ends)
            wos = [w_out[:lru_w], w_out[lru_w:]]
            if not last:
                cs = _tail(lhs_c, wos, cs, *norm_args, bsz * ctx_len, ctx_cond)
            xs = _tail(lhs_x, wos, xs, *norm_args, seq, 0)
        else:
            o = l // 2
            w_qkv = w_qkv_na[o].astype(BF16)
            n_heads = rpb_na.shape[1]
            scale = (d // n_heads) ** -0.5
            splits = ((0, d, scale), (d, 2 * d, 1.0), (2 * d, 3 * d, 1.0))
            dts = (BF16, BF16, BF16)
            q, k, v = _inproj(xs, g_pre_mix[l], sh1, sc1, w_qkv, splits, dts, seq, 0)
            qc, kc, vc = _inproj(cs, g_pre_mix[l], sh1, sc1, w_qkv, splits, dts, bsz * ctx_len, ctx_cond)
            grid = (bsz, seq // GRID_W, GRID_W, d)
            att = _natten(q.reshape(grid), k.reshape(grid), v.reshape(grid),
                          kc.reshape(bsz, ctx_len, d), vc.reshape(bsz, ctx_len, d),
                          _natten_bias_table(rpb_na[o]))
            if not last:
                raise NotImplementedError("context update after an attention layer")
            xs = _tail([att.reshape(bsz * seq, d)], [w_out_na[o].astype(BF16)], xs, *norm_args, seq, 0)
    return xs.reshape(bsz, seq, d)
```

```python
import functools
import math

import numpy as np
import jax
import jax.numpy as jnp
from jax import lax
from jax.experimental import pallas as pl
from jax.experimental.pallas import tpu as pltpu

F32 = jnp.float32
BF16 = jnp.bfloat16

EPS = 1e-6
LRU_C = 8.0
CONV_LEFT = 2
FFT_GROUPS = 8
GRID_W = 64
NA_KH = 8
NA_KW = 16
DFT_RADIX = 8
NEG = -1e30
LOG2E = math.log2(math.e)

VMEM_LIMIT_BYTES = 56 * 1024 * 1024
ROW_TILE = 512
LRU_TILE = 512
LANES = 128
MXU_DIM = 256


def _cparams(*sem):
    return pltpu.CompilerParams(dimension_semantics=sem, vmem_limit_bytes=VMEM_LIMIT_BYTES)


def _const_spec(shape):
    nd = len(shape)
    return pl.BlockSpec(shape, lambda *_: (0,) * nd, pipeline_mode=pl.Buffered(1))


def _sigmoid(x):
    return 0.5 * (1.0 + jnp.tanh(0.5 * x))


def _silu(x):
    return x * _sigmoid(x)


def _gelu_tanh(x):
    c = math.sqrt(2.0 / math.pi)
    return 0.5 * x * (1.0 + jnp.tanh(c * (x + 0.044715 * (x * x * x))))


def _rms(x, g):
    return x * lax.rsqrt(jnp.mean(x * x, axis=-1, keepdims=True) + EPS) * g


def _mod_kernel(cond_ref, w_ref, b_ref, o_ref):
    act = _silu(cond_ref[...]).astype(BF16)
    o_ref[0] = jnp.dot(act, w_ref[0].astype(BF16), preferred_element_type=F32) + b_ref[0]


def _mod_vectors(cond, w_mod, b_mod):
    depth, d, n = w_mod.shape
    tn = n // 4
    return pl.pallas_call(
        _mod_kernel,
        grid=(depth, n // tn),
        in_specs=[pl.BlockSpec(cond.shape, lambda l, j: (0, 0)),
                  pl.BlockSpec((1, d, tn), lambda l, j: (l, 0, j)),
                  pl.BlockSpec((1, 1, tn), lambda l, j: (l, 0, j))],
        out_specs=pl.BlockSpec((1, cond.shape[0], tn), lambda l, j: (l, 0, j)),
        out_shape=jax.ShapeDtypeStruct((depth, cond.shape[0], n), F32),
        compiler_params=_cparams("parallel", "parallel"),
        name="mod",
    )(cond, w_mod, b_mod.reshape(depth, 1, n))


def _inproj_kernel(x_ref, g_ref, sh_ref, sc_ref, w_ref, *rest, splits, transposed):
    h = _rms(x_ref[...], g_ref[...]) * (1.0 + sc_ref[0]) + sh_ref[0]
    hb = h.astype(BF16)
    o_refs = rest
    if transposed:
        wt_ref, o_refs, ot_ref = rest[0], rest[1:-1], rest[-1]
        ot_ref[...] = lax.dot_general(wt_ref[...], hb, (((1,), (1,)), ((), ())),
                                      preferred_element_type=F32).astype(ot_ref.dtype)
    for o_ref, (lo, hi, scale) in zip(o_refs, splits):
        z = jnp.dot(hb, w_ref[:, lo:hi], preferred_element_type=F32)
        if scale != 1.0:
            z = z * scale
        if len(o_ref.shape) == 3:
            for j in range(o_ref.shape[0]):
                o_ref[j] = z[:, j * LANES:(j + 1) * LANES].astype(o_ref.dtype)
        else:
            o_ref[...] = z.astype(o_ref.dtype)


def _inproj(x2, g, sh, sc, w, splits, dtypes, rows_per_cond, cond_base, slabbed=(), wt=None):
    n, d = x2.shape
    tm = min(ROW_TILE, n)
    tiles_per_cond = rows_per_cond // tm
    cond_map = lambda i: (cond_base + i // tiles_per_cond, 0, 0)
    out_specs, out_shape = [], []
    for j, ((lo, hi, _), dt) in enumerate(zip(splits, dtypes)):
        if j in slabbed:
            ns = (hi - lo) // LANES
            out_specs.append(pl.BlockSpec((ns, tm, LANES), lambda i: (0, i, 0)))
            out_shape.append(jax.ShapeDtypeStruct((ns, n, LANES), dt))
        else:
            out_specs.append(pl.BlockSpec((tm, hi - lo), lambda i: (i, 0)))
            out_shape.append(jax.ShapeDtypeStruct((n, hi - lo), dt))
    in_specs = [pl.BlockSpec((tm, d), lambda i: (i, 0)),
                _const_spec((1, d)),
                pl.BlockSpec((1, 1, d), cond_map),
                pl.BlockSpec((1, 1, d), cond_map),
                _const_spec(w.shape)]
    args = [x2, g.reshape(1, d), sh, sc, w]
    if wt is not None:
        in_specs.append(_const_spec(wt.shape))
        args.append(wt)
        out_specs.append(pl.BlockSpec((wt.shape[0], tm), lambda i: (0, i)))
        out_shape.append(jax.ShapeDtypeStruct((wt.shape[0], n), BF16))
    return pl.pallas_call(
        functools.partial(_inproj_kernel, splits=splits, transposed=wt is not None),
        grid=(n // tm,),
        in_specs=in_specs,
        out_specs=out_specs,
        out_shape=out_shape,
        compiler_params=_cparams("parallel"),
        name="inproj",
    )(*args)


def _lru_kernel(*refs, reverse, combine, tt, nt):
    if combine:
        (up_ref, u_ref, un_ref, h0_ref, cw_ref, cb_ref, wa_ref, wi_ref, ba_ref, bi_ref, lam_ref,
         ro_ref, g_ref, out_ref, hl_ref, ext_s, a_s, b_s, h_s) = refs
    else:
        (up_ref, u_ref, un_ref, h0_ref, cw_ref, cb_ref, wa_ref, wi_ref, ba_ref, bi_ref, lam_ref,
         out_ref, hl_ref, ext_s, a_s, b_s, h_s) = refs
    step = pl.program_id(1)
    tile = nt - 1 - step if reverse else step
    c = u_ref.shape[-1]
    half = c // 2

    @pl.when(step == 0)
    def _():
        h_s[...] = h0_ref[0, 0:1, :]

    ext_s[0:8, :] = jnp.where(tile > 0, up_ref[0], 0.0)
    ext_s[8:8 + tt, :] = u_ref[0]
    ext_s[8 + tt:16 + tt, :] = jnp.where(tile < nt - 1, un_ref[0], 0.0)
    uc = cb_ref[...]
    for k in range(cw_ref.shape[0]):
        off = 8 - CONV_LEFT + k
        uc = uc + ext_s[off:off + tt, :] * cw_ref[k:k + 1, :]

    ub = uc.astype(BF16)
    lam = lam_ref[...]
    softplus_neg_lam = jnp.maximum(-lam, 0.0) + jnp.log1p(jnp.exp(-jnp.abs(lam)))
    for hb in range(2):
        sl = slice(hb * half, (hb + 1) * half)
        r = _sigmoid(jnp.dot(ub[:, sl], wa_ref[hb], preferred_element_type=F32) + ba_ref[:, sl])
        i = _sigmoid(jnp.dot(ub[:, sl], wi_ref[hb], preferred_element_type=F32) + bi_ref[:, sl])
        log_a = (-LRU_C) * r * softplus_neg_lam[:, sl]
        a = jnp.exp(log_a)
        a_s[:, sl] = a
        b_s[:, sl] = jnp.sqrt(-jnp.tanh(log_a) * (a * a + 1.0)) * (i * uc[:, sl])

    row = lax.broadcasted_iota(jnp.int32, (8, c), 0)
    ng = tt // 8

    def group(jj, h):
        j = ng - 1 - jj if reverse else jj
        off = pl.multiple_of(j * 8, 8)
        a = a_s[pl.ds(off, 8), :]
        b = b_s[pl.ds(off, 8), :]
        for k in (1, 2, 4):
            shift = 8 - k if reverse else k
            keep = (row < 8 - k) if reverse else (row >= k)
            a_sh = pltpu.roll(a, shift, 0)
            b_sh = pltpu.roll(b, shift, 0)
            b = jnp.where(keep, a * b_sh + b, b)
            a = jnp.where(keep, a * a_sh, a)
        hj = b + a * h
        b_s[pl.ds(off, 8), :] = hj
        return hj[0:1, :] if reverse else hj[7:8, :]

    h_last = lax.fori_loop(0, ng, group, h_s[...], unroll=4)
    h_s[...] = h_last
    hl_ref[0] = jnp.broadcast_to(h_last, hl_ref.shape[1:])
    if combine:
        out_ref[0] = ((ro_ref[0] + b_s[...]) * _gelu_tanh(g_ref[0])).astype(out_ref.dtype)
    else:
        out_ref[0] = b_s[...]


def _lru_direction(u, h0, cw, cb, wa_bd, wi_bd, ba, bi, lam, reverse, other=None, gate=None):
    bsz, t, c = u.shape
    tt = min(LRU_TILE, t)
    nt = t // tt
    g8 = tt // 8
    combine = other is not None
    tile = (lambda s: nt - 1 - s) if reverse else (lambda s: s)
    cur = pl.BlockSpec((1, tt, c), lambda b, s: (b, tile(s), 0))
    in_specs = [pl.BlockSpec((1, 8, c), lambda b, s: (b, jnp.maximum(tile(s) * g8 - 1, 0), 0)),
                cur,
                pl.BlockSpec((1, 8, c), lambda b, s: (b, jnp.minimum((tile(s) + 1) * g8, t // 8 - 1), 0)),
                pl.BlockSpec((1, 8, c), lambda b, s: (b, 0, 0)),
                _const_spec(cw.shape), _const_spec((1, c)),
                _const_spec(wa_bd.shape), _const_spec(wi_bd.shape),
                _const_spec((1, c)), _const_spec((1, c)), _const_spec((1, c))]
    args = [u, u, u, h0, cw, cb.reshape(1, c), wa_bd, wi_bd, ba.reshape(1, c), bi.reshape(1, c),
            lam.reshape(1, c)]
    if combine:
        in_specs += [cur, cur]
        args += [other, gate]
    return pl.pallas_call(
        functools.partial(_lru_kernel, reverse=reverse, combine=combine, tt=tt, nt=nt),
        grid=(bsz, nt),
        in_specs=in_specs,
        out_specs=[cur, pl.BlockSpec((1, 8, c), lambda b, s: (b, 0, 0))],
        out_shape=[jax.ShapeDtypeStruct((bsz, t, c), BF16 if combine else F32),
                   jax.ShapeDtypeStruct((bsz, 8, c), F32)],
        scratch_shapes=[pltpu.VMEM((tt + 16, c), F32), pltpu.VMEM((tt, c), F32),
                        pltpu.VMEM((tt, c), F32), pltpu.VMEM((1, c), F32)],
        compiler_params=_cparams("parallel", "arbitrary"),
        name="lru_bwd" if reverse else "lru_fwd",
    )(*args)


def _block_diag_halves(w):
    h, d, _ = w.shape
    per = h // 2
    eye = jnp.eye(per, dtype=w.dtype)
    wh = w.reshape(2, per, d, d)
    bd = jnp.einsum("pq,cpij->cpiqj", eye, wh).reshape(2, per * d, per * d)
    return bd.astype(BF16)


def _fourier_kernel(x_ref, cs_ref, tc_ref, ts_ref, cc_ref, sc_ref, out_ref, y_s, *, n2):
    radix, _, cw = y_s.shape
    for n1 in range(radix):
        x = jnp.concatenate([x_ref[j, 0, pl.ds(n1, n2, stride=radix), :] for j in range(x_ref.shape[0])],
                            axis=-1)
        y_s[n1] = jnp.dot(cs_ref[...], x.astype(BF16), preferred_element_type=F32)

    def group(g, carry):
        r = pl.multiple_of(g * 8, 8)
        for half in range(cw // LANES):
            ls = slice(half * LANES, (half + 1) * LANES)
            zr, zi = [], []
            for n1 in range(radix):
                yc = y_s[n1, pl.ds(r, 8), ls]
                ys = y_s[n1, pl.ds(n2 + r, 8), ls]
                if n1 == 0:
                    zr.append(yc)
                    zi.append(-ys)
                else:
                    tc = tc_ref[n1, pl.ds(r, 8), :]
                    ts = ts_ref[n1, pl.ds(r, 8), :]
                    zr.append(yc * tc - ys * ts)
                    zi.append(-(yc * ts + ys * tc))
            ar, ai = _dft8(zr, zi)
            for k1 in range(radix):
                y_s[k1, pl.ds(r, 8), ls] = ar[k1]
                y_s[k1, pl.ds(n2 + r, 8), ls] = ai[k1]
        return carry

    lax.fori_loop(0, n2 // 8, group, 0)

    for k1 in range(radix):
        o = (jnp.dot(y_s[k1, 0:n2, :].astype(BF16), cc_ref[...], preferred_element_type=F32)
             + jnp.dot(y_s[k1, n2:2 * n2, :].astype(BF16), sc_ref[...], preferred_element_type=F32))
        out_ref[0, k1 * n2:(k1 + 1) * n2, :] = o.astype(out_ref.dtype)


def _dft8(zr, zi):
    h = math.sqrt(0.5)

    def dft4(r, i):
        s0r, s0i = r[0] + r[2], i[0] + i[2]
        s1r, s1i = r[0] - r[2], i[0] - i[2]
        s2r, s2i = r[1] + r[3], i[1] + i[3]
        s3r, s3i = r[1] - r[3], i[1] - i[3]
        return ([s0r + s2r, s1r + s3i, s0r - s2r, s1r - s3i],
                [s0i + s2i, s1i - s3r, s0i - s2i, s1i + s3r])

    er, ei = dft4(zr[0::2], zi[0::2])
    odr, odi = dft4(zr[1::2], zi[1::2])
    tr = [odr[0], h * (odr[1] + odi[1]), odi[2], h * (odi[3] - odr[3])]
    ti = [odi[0], h * (odi[1] - odr[1]), -odr[2], -h * (odr[3] + odi[3])]
    return ([er[k] + tr[k] for k in range(4)] + [er[k] - tr[k] for k in range(4)],
            [ei[k] + ti[k] for k in range(4)] + [ei[k] - ti[k] for k in range(4)])


def _fourier(f, group_dim):
    ns, bsz, t, _ = f.shape
    c = ns * LANES
    radix = DFT_RADIX
    n2 = t // radix
    cw = MXU_DIM
    nh = c // cw
    spb = cw // LANES
    k = np.arange(n2)
    ang = 2.0 * np.pi * ((k[:, None] * k[None, :]) % n2) / n2
    cs = np.concatenate([np.cos(ang), np.sin(ang)], 0) / math.sqrt(t)
    n1 = np.arange(radix)
    ang_t = 2.0 * np.pi * ((n1[:, None] * k[None, :]) % t) / t
    tw_shape = (radix, n2, LANES)
    j = np.arange(cw)
    same = (j[:, None] // group_dim) == (j[None, :] // group_dim)
    ang_c = 2.0 * np.pi * ((j[:, None] * j[None, :]) % group_dim) / group_dim
    ccm = np.where(same, np.cos(ang_c), 0.0) / math.sqrt(group_dim)
    scm = np.where(same, np.sin(ang_c), 0.0) / math.sqrt(group_dim)
    assert radix == 8, "_dft8 is the radix stage"
    return pl.pallas_call(
        functools.partial(_fourier_kernel, n2=n2),
        grid=(bsz, nh),
        in_specs=[pl.BlockSpec((spb, 1, t, LANES), lambda b, h: (h, b, 0, 0)),
                  _const_spec((2 * n2, n2)),
                  _const_spec(tw_shape), _const_spec(tw_shape),
                  _const_spec((cw, cw)), _const_spec((cw, cw))],
        out_specs=pl.BlockSpec((1, t, cw), lambda b, h: (b, 0, h), pipeline_mode=pl.Buffered(1)),
        out_shape=jax.ShapeDtypeStruct((bsz, t, c), BF16),
        scratch_shapes=[pltpu.VMEM((radix, 2 * n2, cw), F32)],
        compiler_params=_cparams("parallel", "parallel"),
        name="fourier",
    )(f,
      jnp.asarray(cs, F32).astype(BF16),
      jnp.broadcast_to(jnp.asarray(np.cos(ang_t), F32)[:, :, None], tw_shape),
      jnp.broadcast_to(jnp.asarray(np.sin(ang_t), F32)[:, :, None], tw_shape),
      jnp.asarray(ccm, F32).astype(BF16), jnp.asarray(scm, F32).astype(BF16))


def _natten_kernel(q_ref, kt_ref, v_ref, kct_ref, vc_ref, tz_ref, o_ref, bias_s, *, rows, kh):
    rb = pl.program_id(2)
    nrb = pl.num_programs(2)
    qr_n, w, lanes = q_ref.shape[1:]
    hd = lanes // 2
    half_n = qr_n // 2
    band = half_n + kh
    nq = half_n * w
    lane = lax.broadcasted_iota(jnp.int32, (1, lanes), 1)
    first = lane < hd

    def band_start(r_first):
        return jnp.clip(r_first - kh // 2, 0, rows - band)

    @pl.when((rb <= 1) | (rb == nrb - 1))
    def _():
        for half in range(2):
            r_first = rb * qr_n + half * half_n
            kr0 = band_start(r_first)
            for qr in range(half_n):
                r = r_first + qr
                rs = jnp.clip(r - kh // 2, 0, rows - kh)
                for m in range(band // 2):
                    kr = kr0 + 2 * m
                    ae = kr - r + (NA_KH - 1) + NA_KH
                    kr_lane = kr + lane // hd
                    ok = (kr_lane >= rs) & (kr_lane < rs + kh)
                    for hh in range(2):
                        row0 = hh * nq + qr * w
                        bias_s[half, row0:row0 + w, m * lanes:(m + 1) * lanes] = jnp.where(
                            ok, tz_ref[hh, ae], NEG)

    ones = jnp.ones((1, lanes), BF16)
    mask = [(lane // hd == hh).astype(F32).astype(BF16) for hh in range(2)]
    vc = vc_ref[0]
    vc1 = jnp.concatenate([vc, jnp.broadcast_to(ones, vc.shape)], axis=1)
    for half in range(2):
        r_first = rb * qr_n + half * half_n
        kr0 = band_start(r_first)
        q = q_ref[0, half * half_n:(half + 1) * half_n].reshape(nq, lanes)
        qq = jnp.concatenate([q * mask[0], q * mask[1]], axis=0)
        kt = kt_ref[:, pl.ds(pl.multiple_of(kr0 * w, MXU_DIM), band * w)]
        vb = v_ref[0, pl.ds(kr0, band)].reshape(band * w, lanes)
        s_loc = jnp.dot(qq, kt, preferred_element_type=F32) + bias_s[half]
        s_ctx = jnp.dot(qq, kct_ref[...], preferred_element_type=F32)
        mx = jnp.maximum(jnp.max(s_loc, axis=-1, keepdims=True), jnp.max(s_ctx, axis=-1, keepdims=True))
        p_loc = jnp.exp2((s_loc - mx).astype(BF16))
        p_ctx = jnp.exp2((s_ctx - mx).astype(BF16))
        vb1 = jnp.concatenate([vb, jnp.broadcast_to(ones, vb.shape)], axis=1)
        acc = (jnp.dot(p_loc, vb1, preferred_element_type=F32)
               + jnp.dot(p_ctx, vc1, preferred_element_type=F32))
        o0 = acc[:nq, :lanes] / acc[:nq, lanes:]
        o1 = acc[nq:, :lanes] / acc[nq:, lanes:]
        o_ref[0, half * half_n:(half + 1) * half_n] = jnp.where(first, o0, o1).reshape(
            half_n, w, lanes).astype(o_ref.dtype)


def _natten_bias_table(rpb):
    h = rpb.shape[0]
    w = GRID_W
    qc = np.arange(w)
    cs = np.clip(qc - NA_KW // 2, 0, w - NA_KW)
    kc = np.arange(w)
    inside = (kc[None, :] >= cs[:, None]) & (kc[None, :] < cs[:, None] + NA_KW)
    dc = np.clip(kc[None, :] - qc[:, None] + NA_KW - 1, 0, 2 * NA_KW - 2)
    slabs = jnp.where(inside[None, None], LOG2E * rpb[:, :, dc], NEG)
    n_slab = 4 * NA_KH + 1
    pad_lo = NA_KH
    pad_hi = n_slab - pad_lo - slabs.shape[1]
    slabs = jnp.pad(slabs, ((0, 0), (pad_lo, pad_hi), (0, 0), (0, 0)), constant_values=NEG)
    return jnp.concatenate([slabs[:, :-1], slabs[:, 1:]], axis=-1)


def _natten(q, kt, v, kct, vc, table):
    bsz, rows, w, d = q.shape
    l = vc.shape[1]
    qr_n = NA_KH
    kh = min(NA_KH, rows)
    band = qr_n // 2 + kh
    assert w == GRID_W and 2 * w == LANES and rows >= 2 * qr_n and rows % qr_n == 0
    assert (qr_n // 2 * w) % MXU_DIM == 0 and l % LANES == 0
    nhp = d // LANES
    blk = pl.BlockSpec((1, qr_n, w, LANES), lambda b, hp, rb: (b, rb, 0, hp))
    return pl.pallas_call(
        functools.partial(_natten_kernel, rows=rows, kh=kh),
        grid=(bsz, nhp, rows // qr_n),
        in_specs=[blk,
                  pl.BlockSpec((LANES, rows * w), lambda b, hp, rb: (hp, b)),
                  pl.BlockSpec((1, rows, w, LANES), lambda b, hp, rb: (b, 0, 0, hp)),
                  pl.BlockSpec((LANES, l), lambda b, hp, rb: (hp, b)),
                  pl.BlockSpec((1, l, LANES), lambda b, hp, rb: (b, 0, hp)),
                  pl.BlockSpec((2,) + table.shape[1:], lambda b, hp, rb: (hp, 0, 0, 0))],
        out_specs=blk,
        out_shape=jax.ShapeDtypeStruct(q.shape, BF16),
        scratch_shapes=[pltpu.VMEM((2, qr_n * w, band * w), F32)],
        compiler_params=_cparams("parallel", "parallel", "arbitrary"),
        name="natten",
    )(q, kt, v, kct, vc, table)


def _tail_kernel(*refs, n_lhs, f_chunk):
    lhs = refs[:n_lhs]
    wos = refs[n_lhs:2 * n_lhs]
    (x_ref, gpm_ref, gt1_ref, gpf_ref, sh2_ref, sc2_ref, wg_ref, wu_ref, wd_ref, gpo_ref, gt2_ref,
     o_ref) = refs[2 * n_lhs:]
    y = None
    for a_ref, w_ref in zip(lhs, wos):
        part = jnp.dot(a_ref[...], w_ref[...], preferred_element_type=F32)
        y = part if y is None else y + part
    x1 = x_ref[...] + gt1_ref[0] * _rms(y, gpm_ref[...])
    fx = (_rms(x1, gpf_ref[...]) * (1.0 + sc2_ref[0]) + sh2_ref[0]).astype(BF16)
    ffn = None
    for lo in range(0, wg_ref.shape[1], f_chunk):
        gate = jnp.dot(fx, wg_ref[:, lo:lo + f_chunk], preferred_element_type=F32)
        up = jnp.dot(fx, wu_ref[:, lo:lo + f_chunk], preferred_element_type=F32)
        hid = (_silu(gate) * up).astype(BF16)
        part = jnp.dot(hid, wd_ref[lo:lo + f_chunk, :], preferred_element_type=F32)
        ffn = part if ffn is None else ffn + part
    o_ref[...] = x1 + gt2_ref[0] * _rms(ffn, gpo_ref[...])


def _tail(lhs, wos, x2, gpm, gt1, gpf, sh2, sc2, wg, wu, wd, gpo, gt2, rows_per_cond, cond_base):
    n, d = x2.shape
    f = wg.shape[1]
    tm = min(ROW_TILE, n)
    tiles_per_cond = rows_per_cond // tm
    cond_map = lambda i: (cond_base + i // tiles_per_cond, 0, 0)
    cond = pl.BlockSpec((1, 1, d), cond_map)
    vec = _const_spec((1, d))
    return pl.pallas_call(
        functools.partial(_tail_kernel, n_lhs=len(lhs), f_chunk=f // 2),
        grid=(n // tm,),
        in_specs=([pl.BlockSpec((tm, a.shape[1]), lambda i: (i, 0)) for a in lhs]
                  + [_const_spec(w.shape) for w in wos]
                  + [pl.BlockSpec((tm, d), lambda i: (i, 0)), vec, cond, vec, cond, cond,
                     _const_spec(wg.shape), _const_spec(wu.shape), _const_spec(wd.shape), vec, cond]),
        out_specs=pl.BlockSpec((tm, d), lambda i: (i, 0)),
        out_shape=jax.ShapeDtypeStruct((n, d), F32),
        compiler_params=_cparams("parallel"),
        name="tail",
    )(*lhs, *wos, x2, gpm.reshape(1, d), gt1, gpf.reshape(1, d), sh2, sc2, wg, wu, wd,
      gpo.reshape(1, d), gt2)


def kernel(x, c, ctx, c_ctx, w_mod, b_mod, g_pre_mix, g_post_mix, g_pre_ffn, g_post_ffn, w_ffn_gate, w_ffn_up, w_ffn_down, w_in_ab, conv_w, conv_b, lru_w_a, lru_b_a, lru_w_i, lru_b_i, lru_lam, w_out_ab, w_qkv_na, rpb_na, w_out_na):
    bsz, seq, d = x.shape
    ctx_len = ctx.shape[1]
    depth = w_mod.shape[0]
    lru_w = conv_w.shape[-1]
    n_cond = 8
    assert bsz < n_cond
    ctx_cond = bsz

    cond = jnp.concatenate([c, c_ctx[None], jnp.zeros((n_cond - bsz - 1, d), F32)], 0)
    mod = _mod_vectors(cond, w_mod, b_mod).reshape(depth, n_cond, 6, 1, d)

    xs = x.reshape(bsz * seq, d)
    cs = ctx.reshape(bsz * ctx_len, d)
    for l in range(depth):
        last = l == depth - 1
        sh1, sc1, gt1, sh2, sc2, gt2 = (mod[l, :, j] for j in range(6))
        wg, wu, wd = (w[l].astype(BF16) for w in (w_ffn_gate, w_ffn_up, w_ffn_down))
        norm_args = (g_post_mix[l], gt1, g_pre_ffn[l], sh2, sc2, wg, wu, wd, g_post_ffn[l], gt2)
        if l % 2 == 0:
            e = l // 2
            w_in = w_in_ab[e].astype(BF16)
            w_out = w_out_ab[e].astype(BF16)
            splits = ((0, lru_w, 1.0), (lru_w, 2 * lru_w, 1.0), (2 * lru_w, w_in.shape[1], 1.0))
            gates = [(_block_diag_halves(lru_w_a[e, dr]), _block_diag_halves(lru_w_i[e, dr]))
                     for dr in range(2)]

            def mixer(tokens, t_len, rows_per_cond, cond_base, h0):
                u, g, f = _inproj(tokens, g_pre_mix[l], sh1, sc1, w_in, splits, (F32, F32, F32),
                                  rows_per_cond, cond_base, slabbed=(2,))
                u, g = (a.reshape(bsz, t_len, -1) for a in (u, g))
                f = f.reshape(f.shape[0], bsz, t_len, LANES)
                lru = functools.partial(_lru_direction, u, cw=conv_w[e], cb=conv_b[e])
                hf, end_f = lru(h0[0], wa_bd=gates[0][0], wi_bd=gates[0][1], ba=lru_b_a[e, 0],
                                bi=lru_b_i[e, 0], lam=lru_lam[e, 0], reverse=False)
                lr, end_b = lru(h0[1], wa_bd=gates[1][0], wi_bd=gates[1][1], ba=lru_b_a[e, 1],
                                bi=lru_b_i[e, 1], lam=lru_lam[e, 1], reverse=True, other=hf, gate=g)
                fo = _fourier(f, f.shape[0] * LANES // FFT_GROUPS)
                lhs = [lr.reshape(tokens.shape[0], -1), fo.reshape(tokens.shape[0], -1)]
                return lhs, (end_f, end_b)

            zeros = jnp.zeros((bsz, 8, lru_w), F32)
            lhs_c, ends = mixer(cs, ctx_len, bsz * ctx_len, ctx_cond, (zeros, zeros))
            lhs_x, _ = mixer(xs, seq, seq, 0, ends)
            wos = [w_out[:lru_w], w_out[lru_w:]]
            if not last:
                cs = _tail(lhs_c, wos, cs, *norm_args, bsz * ctx_len, ctx_cond)
            xs = _tail(lhs_x, wos, xs, *norm_args, seq, 0)
        else:
            o = l // 2
            w_qkv = w_qkv_na[o].astype(BF16)
            n_heads = rpb_na.shape[1]
            scale = LOG2E * (d // n_heads) ** -0.5
            q_split, v_split = (0, d, scale), (2 * d, 3 * d, 1.0)
            w_k_t = w_qkv[:, d:2 * d].T
            q, v, kt = _inproj(xs, g_pre_mix[l], sh1, sc1, w_qkv, (q_split, v_split), (BF16, BF16),
                               seq, 0, wt=w_k_t)
            vc, kct = _inproj(cs, g_pre_mix[l], sh1, sc1, w_qkv, (v_split,), (BF16,),
                              bsz * ctx_len, ctx_cond, wt=w_k_t)
            grid = (bsz, seq // GRID_W, GRID_W, d)
            att = _natten(q.reshape(grid), kt, v.reshape(grid), kct, vc.reshape(bsz, ctx_len, d),
                          _natten_bias_table(rpb_na[o]))
            if not last:
                raise NotImplementedError("context update after an attention layer")
            xs = _tail([att.reshape(bsz * seq, d)], [w_out_na[o].astype(BF16)], xs, *norm_args, seq, 0)
    return xs.reshape(bsz, seq, d)
```

```python
import functools
import math

import numpy as np
import jax
import jax.numpy as jnp
from jax import lax
from jax.experimental import pallas as pl
from jax.experimental.pallas import tpu as pltpu

F32 = jnp.float32
BF16 = jnp.bfloat16

EPS = 1e-6
LRU_C = 8.0
CONV_LEFT = 2
FFT_GROUPS = 8
GRID_W = 64
NA_KH = 8
NA_KW = 16
DFT_RADIX = 8
NEG = -1e30
LOG2E = math.log2(math.e)

VMEM_LIMIT_BYTES = 56 * 1024 * 1024
ROW_TILE = 1024
LRU_TILE = 512
TAIL_TILE = 1024
TAIL_F_CHUNK = 256
LANES = 128
MXU_DIM = 256


def _cparams(*sem):
    return pltpu.CompilerParams(dimension_semantics=sem, vmem_limit_bytes=VMEM_LIMIT_BYTES)


def _const_spec(shape):
    nd = len(shape)
    return pl.BlockSpec(shape, lambda *_: (0,) * nd, pipeline_mode=pl.Buffered(1))


def _sigmoid(x):
    return 0.5 * (1.0 + jnp.tanh(0.5 * x))


def _silu(x):
    return x * _sigmoid(x)


def _gelu_tanh(x):
    c = math.sqrt(2.0 / math.pi)
    return 0.5 * x * (1.0 + jnp.tanh(c * (x + 0.044715 * (x * x * x))))


def _rms(x, g):
    return x * lax.rsqrt(jnp.mean(x * x, axis=-1, keepdims=True) + EPS) * g


def _mod_kernel(cond_ref, w_ref, b_ref, o_ref):
    act = _silu(cond_ref[...]).astype(BF16)
    o_ref[0] = jnp.dot(act, w_ref[0].astype(BF16), preferred_element_type=F32) + b_ref[0]


def _mod_vectors(cond, w_mod, b_mod):
    depth, d, n = w_mod.shape
    tn = n // 4
    return pl.pallas_call(
        _mod_kernel,
        grid=(depth, n // tn),
        in_specs=[pl.BlockSpec(cond.shape, lambda l, j: (0, 0)),
                  pl.BlockSpec((1, d, tn), lambda l, j: (l, 0, j)),
                  pl.BlockSpec((1, 1, tn), lambda l, j: (l, 0, j))],
        out_specs=pl.BlockSpec((1, cond.shape[0], tn), lambda l, j: (l, 0, j)),
        out_shape=jax.ShapeDtypeStruct((depth, cond.shape[0], n), F32),
        compiler_params=_cparams("parallel", "parallel"),
        name="mod",
    )(cond, w_mod, b_mod.reshape(depth, 1, n))


def _inproj_kernel(x_ref, g_ref, sh_ref, sc_ref, w_ref, *rest, splits, transposed):
    o_refs = rest
    if transposed:
        wt_ref, o_refs, ot_ref = rest[0], rest[1:-1], rest[-1]
    hm = x_ref.shape[0] // 2

    def norm(rs):
        h = _rms(x_ref[rs, :], g_ref[...]) * (1.0 + sc_ref[0]) + sh_ref[0]
        return h.astype(BF16)

    def project(rs, hb):
        if transposed:
            ot_ref[:, rs] = lax.dot_general(wt_ref[...], hb, (((1,), (1,)), ((), ())),
                                            preferred_element_type=F32).astype(ot_ref.dtype)
        for o_ref, (lo, hi, scale) in zip(o_refs, splits):
            z = jnp.dot(hb, w_ref[:, lo:hi], preferred_element_type=F32)
            if scale != 1.0:
                z = z * scale
            if len(o_ref.shape) == 3:
                for j in range(o_ref.shape[0]):
                    o_ref[j, rs, :] = z[:, j * LANES:(j + 1) * LANES].astype(o_ref.dtype)
            else:
                o_ref[rs, :] = z.astype(o_ref.dtype)

    ra, rb = slice(0, hm), slice(hm, 2 * hm)
    ha = norm(ra)
    hb = norm(rb)
    project(ra, ha)
    project(rb, hb)


def _inproj(x2, g, sh, sc, w, splits, dtypes, rows_per_cond, cond_base, slabbed=(), wt=None):
    n, d = x2.shape
    tm = min(ROW_TILE, n)
    tiles_per_cond = rows_per_cond // tm
    cond_map = lambda i: (cond_base + i // tiles_per_cond, 0, 0)
    out_specs, out_shape = [], []
    for j, ((lo, hi, _), dt) in enumerate(zip(splits, dtypes)):
        if j in slabbed:
            ns = (hi - lo) // LANES
            out_specs.append(pl.BlockSpec((ns, tm, LANES), lambda i: (0, i, 0)))
            out_shape.append(jax.ShapeDtypeStruct((ns, n, LANES), dt))
        else:
            out_specs.append(pl.BlockSpec((tm, hi - lo), lambda i: (i, 0)))
            out_shape.append(jax.ShapeDtypeStruct((n, hi - lo), dt))
    in_specs = [pl.BlockSpec((tm, d), lambda i: (i, 0)),
                _const_spec((1, d)),
                pl.BlockSpec((1, 1, d), cond_map),
                pl.BlockSpec((1, 1, d), cond_map),
                _const_spec(w.shape)]
    args = [x2, g.reshape(1, d), sh, sc, w]
    if wt is not None:
        in_specs.append(_const_spec(wt.shape))
        args.append(wt)
        out_specs.append(pl.BlockSpec((wt.shape[0], tm), lambda i: (0, i)))
        out_shape.append(jax.ShapeDtypeStruct((wt.shape[0], n), BF16))
    return pl.pallas_call(
        functools.partial(_inproj_kernel, splits=splits, transposed=wt is not None),
        grid=(n // tm,),
        in_specs=in_specs,
        out_specs=out_specs,
        out_shape=out_shape,
        compiler_params=_cparams("parallel"),
        name="inproj",
    )(*args)


def _lru_kernel(*refs, reverse, combine, tt, nt):
    if combine:
        (up_ref, u_ref, un_ref, h0_ref, cw_ref, cb_ref, wa_ref, wi_ref, ba_ref, bi_ref, lam_ref,
         ro_ref, g_ref, out_ref, hl_ref, ext_s, a_s, b_s, h_s) = refs
    else:
        (up_ref, u_ref, un_ref, h0_ref, cw_ref, cb_ref, wa_ref, wi_ref, ba_ref, bi_ref, lam_ref,
         out_ref, hl_ref, ext_s, a_s, b_s, h_s) = refs
    step = pl.program_id(1)
    tile = nt - 1 - step if reverse else step
    c = u_ref.shape[-1]
    half = c // 2

    @pl.when(step == 0)
    def _():
        h_s[...] = h0_ref[0, 0:1, :]

    ext_s[0:8, :] = jnp.where(tile > 0, up_ref[0], 0.0)
    ext_s[8:8 + tt, :] = u_ref[0]
    ext_s[8 + tt:16 + tt, :] = jnp.where(tile < nt - 1, un_ref[0], 0.0)
    uc = cb_ref[...]
    for k in range(cw_ref.shape[0]):
        off = 8 - CONV_LEFT + k
        uc = uc + ext_s[off:off + tt, :] * cw_ref[k:k + 1, :]

    ub = uc.astype(BF16)
    lam = lam_ref[...]
    softplus_neg_lam = jnp.maximum(-lam, 0.0) + jnp.log1p(jnp.exp(-jnp.abs(lam)))
    for hb in range(2):
        sl = slice(hb * half, (hb + 1) * half)
        r = _sigmoid(jnp.dot(ub[:, sl], wa_ref[hb], preferred_element_type=F32) + ba_ref[:, sl])
        i = _sigmoid(jnp.dot(ub[:, sl], wi_ref[hb], preferred_element_type=F32) + bi_ref[:, sl])
        log_a = (-LRU_C) * r * softplus_neg_lam[:, sl]
        a = jnp.exp(log_a)
        a_s[:, sl] = a
        b_s[:, sl] = jnp.sqrt(-jnp.tanh(log_a) * (a * a + 1.0)) * (i * uc[:, sl])

    row = lax.broadcasted_iota(jnp.int32, (8, c), 0)
    ng = tt // 8

    def group(jj, h):
        j = ng - 1 - jj if reverse else jj
        off = pl.multiple_of(j * 8, 8)
        a = a_s[pl.ds(off, 8), :]
        b = b_s[pl.ds(off, 8), :]
        for k in (1, 2, 4):
            shift = 8 - k if reverse else k
            keep = (row < 8 - k) if reverse else (row >= k)
            a_sh = pltpu.roll(a, shift, 0)
            b_sh = pltpu.roll(b, shift, 0)
            b = jnp.where(keep, a * b_sh + b, b)
            a = jnp.where(keep, a * a_sh, a)
        hj = b + a * h
        b_s[pl.ds(off, 8), :] = hj
        return hj[0:1, :] if reverse else hj[7:8, :]

    h_last = lax.fori_loop(0, ng, group, h_s[...], unroll=4)
    h_s[...] = h_last
    hl_ref[0] = jnp.broadcast_to(h_last, hl_ref.shape[1:])
    if combine:
        out_ref[0] = ((ro_ref[0] + b_s[...]) * _gelu_tanh(g_ref[0])).astype(out_ref.dtype)
    else:
        out_ref[0] = b_s[...]


def _lru_direction(u, h0, cw, cb, wa_bd, wi_bd, ba, bi, lam, reverse, other=None, gate=None):
    bsz, t, c = u.shape
    tt = min(LRU_TILE, t)
    nt = t // tt
    g8 = tt // 8
    combine = other is not None
    tile = (lambda s: nt - 1 - s) if reverse else (lambda s: s)
    cur = pl.BlockSpec((1, tt, c), lambda b, s: (b, tile(s), 0))
    in_specs = [pl.BlockSpec((1, 8, c), lambda b, s: (b, jnp.maximum(tile(s) * g8 - 1, 0), 0)),
                cur,
                pl.BlockSpec((1, 8, c), lambda b, s: (b, jnp.minimum((tile(s) + 1) * g8, t // 8 - 1), 0)),
                pl.BlockSpec((1, 8, c), lambda b, s: (b, 0, 0)),
                _const_spec(cw.shape), _const_spec((1, c)),
                _const_spec(wa_bd.shape), _const_spec(wi_bd.shape),
                _const_spec((1, c)), _const_spec((1, c)), _const_spec((1, c))]
    args = [u, u, u, h0, cw, cb.reshape(1, c), wa_bd, wi_bd, ba.reshape(1, c), bi.reshape(1, c),
            lam.reshape(1, c)]
    if combine:
        in_specs += [cur, cur]
        args += [other, gate]
    return pl.pallas_call(
        functools.partial(_lru_kernel, reverse=reverse, combine=combine, tt=tt, nt=nt),
        grid=(bsz, nt),
        in_specs=in_specs,
        out_specs=[cur, pl.BlockSpec((1, 8, c), lambda b, s: (b, 0, 0))],
        out_shape=[jax.ShapeDtypeStruct((bsz, t, c), BF16 if combine else F32),
                   jax.ShapeDtypeStruct((bsz, 8, c), F32)],
        scratch_shapes=[pltpu.VMEM((tt + 16, c), F32), pltpu.VMEM((tt, c), F32),
                        pltpu.VMEM((tt, c), F32), pltpu.VMEM((1, c), F32)],
        compiler_params=_cparams("parallel", "arbitrary"),
        name="lru_bwd" if reverse else "lru_fwd",
    )(*args)


def _block_diag_halves(w):
    h, d, _ = w.shape
    per = h // 2
    eye = jnp.eye(per, dtype=w.dtype)
    wh = w.reshape(2, per, d, d)
    bd = jnp.einsum("pq,cpij->cpiqj", eye, wh).reshape(2, per * d, per * d)
    return bd.astype(BF16)


def _fourier_kernel(x_ref, cs_ref, tc_ref, ts_ref, cc_ref, sc_ref, out_ref, y_s, *, n2):
    radix, _, cw = y_s.shape
    for n1 in range(radix):
        x = jnp.concatenate([x_ref[j, 0, pl.ds(n1, n2, stride=radix), :] for j in range(x_ref.shape[0])],
                            axis=-1)
        y_s[n1] = jnp.dot(cs_ref[...], x.astype(BF16), preferred_element_type=F32)

    def group(g, carry):
        r = pl.multiple_of(g * 8, 8)
        for half in range(cw // LANES):
            ls = slice(half * LANES, (half + 1) * LANES)
            zr, zi = [], []
            for n1 in range(radix):
                yc = y_s[n1, pl.ds(r, 8), ls]
                ys = y_s[n1, pl.ds(n2 + r, 8), ls]
                if n1 == 0:
                    zr.append(yc)
                    zi.append(-ys)
                else:
                    tc = tc_ref[n1, pl.ds(r, 8), :]
                    ts = ts_ref[n1, pl.ds(r, 8), :]
                    zr.append(yc * tc - ys * ts)
                    zi.append(-(yc * ts + ys * tc))
            ar, ai = _dft8(zr, zi)
            for k1 in range(radix):
                y_s[k1, pl.ds(r, 8), ls] = ar[k1]
                y_s[k1, pl.ds(n2 + r, 8), ls] = ai[k1]
        return carry

    lax.fori_loop(0, n2 // 8, group, 0)

    for k1 in range(radix):
        o = (jnp.dot(y_s[k1, 0:n2, :].astype(BF16), cc_ref[...], preferred_element_type=F32)
             + jnp.dot(y_s[k1, n2:2 * n2, :].astype(BF16), sc_ref[...], preferred_element_type=F32))
        out_ref[0, k1 * n2:(k1 + 1) * n2, :] = o.astype(out_ref.dtype)


def _dft8(zr, zi):
    h = math.sqrt(0.5)

    def dft4(r, i):
        s0r, s0i = r[0] + r[2], i[0] + i[2]
        s1r, s1i = r[0] - r[2], i[0] - i[2]
        s2r, s2i = r[1] + r[3], i[1] + i[3]
        s3r, s3i = r[1] - r[3], i[1] - i[3]
        return ([s0r + s2r, s1r + s3i, s0r - s2r, s1r - s3i],
                [s0i + s2i, s1i - s3r, s0i - s2i, s1i + s3r])

    er, ei = dft4(zr[0::2], zi[0::2])
    odr, odi = dft4(zr[1::2], zi[1::2])
    tr = [odr[0], h * (odr[1] + odi[1]), odi[2], h * (odi[3] - odr[3])]
    ti = [odi[0], h * (odi[1] - odr[1]), -odr[2], -h * (odr[3] + odi[3])]
    return ([er[k] + tr[k] for k in range(4)] + [er[k] - tr[k] for k in range(4)],
            [ei[k] + ti[k] for k in range(4)] + [ei[k] - ti[k] for k in range(4)])


def _fourier(f, group_dim):
    ns, bsz, t, _ = f.shape
    c = ns * LANES
    radix = DFT_RADIX
    n2 = t // radix
    cw = MXU_DIM
    nh = c // cw
    spb = cw // LANES
    k = np.arange(n2)
    ang = 2.0 * np.pi * ((k[:, None] * k[None, :]) % n2) / n2
    cs = np.concatenate([np.cos(ang), np.sin(ang)], 0) / math.sqrt(t)
    n1 = np.arange(radix)
    ang_t = 2.0 * np.pi * ((n1[:, None] * k[None, :]) % t) / t
    tw_shape = (radix, n2, LANES)
    j = np.arange(cw)
    same = (j[:, None] // group_dim) == (j[None, :] // group_dim)
    ang_c = 2.0 * np.pi * ((j[:, None] * j[None, :]) % group_dim) / group_dim
    ccm = np.where(same, np.cos(ang_c), 0.0) / math.sqrt(group_dim)
    scm = np.where(same, np.sin(ang_c), 0.0) / math.sqrt(group_dim)
    assert radix == 8, "_dft8 is the radix stage"
    return pl.pallas_call(
        functools.partial(_fourier_kernel, n2=n2),
        grid=(bsz, nh),
        in_specs=[pl.BlockSpec((spb, 1, t, LANES), lambda b, h: (h, b, 0, 0)),
                  _const_spec((2 * n2, n2)),
                  _const_spec(tw_shape), _const_spec(tw_shape),
                  _const_spec((cw, cw)), _const_spec((cw, cw))],
        out_specs=pl.BlockSpec((1, t, cw), lambda b, h: (b, 0, h), pipeline_mode=pl.Buffered(1)),
        out_shape=jax.ShapeDtypeStruct((bsz, t, c), BF16),
        scratch_shapes=[pltpu.VMEM((radix, 2 * n2, cw), F32)],
        compiler_params=_cparams("parallel", "parallel"),
        name="fourier",
    )(f,
      jnp.asarray(cs, F32).astype(BF16),
      jnp.broadcast_to(jnp.asarray(np.cos(ang_t), F32)[:, :, None], tw_shape),
      jnp.broadcast_to(jnp.asarray(np.sin(ang_t), F32)[:, :, None], tw_shape),
      jnp.asarray(ccm, F32).astype(BF16), jnp.asarray(scm, F32).astype(BF16))


def _natten_kernel(q_ref, kt_ref, v_ref, kct_ref, vc_ref, tz_ref, o_ref, bias_s, *, rows, kh):
    rb = pl.program_id(2)
    nrb = pl.num_programs(2)
    qr_n, w, lanes = q_ref.shape[1:]
    hd = lanes // 2
    half_n = qr_n // 2
    band = half_n + kh
    nq = half_n * w
    lane = lax.broadcasted_iota(jnp.int32, (1, lanes), 1)
    first = lane < hd

    def band_start(r_first):
        return jnp.clip(r_first - kh // 2, 0, rows - band)

    @pl.when((rb <= 1) | (rb == nrb - 1))
    def _():
        for half in range(2):
            r_first = rb * qr_n + half * half_n
            kr0 = band_start(r_first)
            for qr in range(half_n):
                r = r_first + qr
                rs = jnp.clip(r - kh // 2, 0, rows - kh)
                for m in range(band // 2):
                    kr = kr0 + 2 * m
                    ae = kr - r + (NA_KH - 1) + NA_KH
                    kr_lane = kr + lane // hd
                    ok = (kr_lane >= rs) & (kr_lane < rs + kh)
                    for hh in range(2):
                        row0 = hh * nq + qr * w
                        bias_s[half, row0:row0 + w, m * lanes:(m + 1) * lanes] = jnp.where(
                            ok, tz_ref[hh, ae], NEG)

    ones = jnp.ones((1, lanes), BF16)
    mask = [(lane // hd == hh).astype(F32).astype(BF16) for hh in range(2)]
    vc = vc_ref[0]
    vc1 = jnp.concatenate([vc, jnp.broadcast_to(ones, vc.shape)], axis=1)
    def scores(half):
        kr0 = band_start(rb * qr_n + half * half_n)
        q = q_ref[0, half * half_n:(half + 1) * half_n].reshape(nq, lanes)
        qq = jnp.concatenate([q * mask[0], q * mask[1]], axis=0)
        kt = kt_ref[:, pl.ds(pl.multiple_of(kr0 * w, MXU_DIM), band * w)]
        s_loc = jnp.dot(qq, kt, preferred_element_type=F32) + bias_s[half]
        s_ctx = jnp.dot(qq, kct_ref[...], preferred_element_type=F32)
        return s_loc, s_ctx

    def softmax(s):
        s_loc, s_ctx = s
        mx = jnp.maximum(jnp.max(s_loc, axis=-1, keepdims=True), jnp.max(s_ctx, axis=-1, keepdims=True))
        return jnp.exp2((s_loc - mx).astype(BF16)), jnp.exp2((s_ctx - mx).astype(BF16))

    def attend(half, p):
        kr0 = band_start(rb * qr_n + half * half_n)
        vb = v_ref[0, pl.ds(kr0, band)].reshape(band * w, lanes)
        vb1 = jnp.concatenate([vb, jnp.broadcast_to(ones, vb.shape)], axis=1)
        acc = (jnp.dot(p[0], vb1, preferred_element_type=F32)
               + jnp.dot(p[1], vc1, preferred_element_type=F32))
        o0 = acc[:nq, :lanes] / acc[:nq, lanes:]
        o1 = acc[nq:, :lanes] / acc[nq:, lanes:]
        o_ref[0, half * half_n:(half + 1) * half_n] = jnp.where(first, o0, o1).reshape(
            half_n, w, lanes).astype(o_ref.dtype)

    s_a = scores(0)
    s_b = scores(1)
    p_a = softmax(s_a)
    attend(0, p_a)
    p_b = softmax(s_b)
    attend(1, p_b)


def _natten_bias_table(rpb):
    h = rpb.shape[0]
    w = GRID_W
    qc = np.arange(w)
    cs = np.clip(qc - NA_KW // 2, 0, w - NA_KW)
    kc = np.arange(w)
    inside = (kc[None, :] >= cs[:, None]) & (kc[None, :] < cs[:, None] + NA_KW)
    dc = np.clip(kc[None, :] - qc[:, None] + NA_KW - 1, 0, 2 * NA_KW - 2)
    slabs = jnp.where(inside[None, None], LOG2E * rpb[:, :, dc], NEG)
    n_slab = 4 * NA_KH + 1
    pad_lo = NA_KH
    pad_hi = n_slab - pad_lo - slabs.shape[1]
    slabs = jnp.pad(slabs, ((0, 0), (pad_lo, pad_hi), (0, 0), (0, 0)), constant_values=NEG)
    return jnp.concatenate([slabs[:, :-1], slabs[:, 1:]], axis=-1)


def _natten(q, kt, v, kct, vc, table):
    bsz, rows, w, d = q.shape
    l = vc.shape[1]
    qr_n = NA_KH
    kh = min(NA_KH, rows)
    band = qr_n // 2 + kh
    assert w == GRID_W and 2 * w == LANES and rows >= 2 * qr_n and rows % qr_n == 0
    assert (qr_n // 2 * w) % MXU_DIM == 0 and l % LANES == 0
    nhp = d // LANES
    blk = pl.BlockSpec((1, qr_n, w, LANES), lambda b, hp, rb: (b, rb, 0, hp))
    return pl.pallas_call(
        functools.partial(_natten_kernel, rows=rows, kh=kh),
        grid=(bsz, nhp, rows // qr_n),
        in_specs=[blk,
                  pl.BlockSpec((LANES, rows * w), lambda b, hp, rb: (hp, b)),
                  pl.BlockSpec((1, rows, w, LANES), lambda b, hp, rb: (b, 0, 0, hp)),
                  pl.BlockSpec((LANES, l), lambda b, hp, rb: (hp, b)),
                  pl.BlockSpec((1, l, LANES), lambda b, hp, rb: (b, 0, hp)),
                  pl.BlockSpec((2,) + table.shape[1:], lambda b, hp, rb: (hp, 0, 0, 0))],
        out_specs=blk,
        out_shape=jax.ShapeDtypeStruct(q.shape, BF16),
        scratch_shapes=[pltpu.VMEM((2, qr_n * w, band * w), F32)],
        compiler_params=_cparams("parallel", "parallel", "arbitrary"),
        name="natten",
    )(q, kt, v, kct, vc, table)


def _tail_kernel(*refs, n_lhs, f_chunk):
    lhs = refs[:n_lhs]
    wos = refs[n_lhs:2 * n_lhs]
    (x_ref, gpm_ref, gt1_ref, gpf_ref, sh2_ref, sc2_ref, wg_ref, wu_ref, wd_ref, gpo_ref, gt2_ref,
     o_ref) = refs[2 * n_lhs:]
    hm = x_ref.shape[0] // 2

    def outproj(rs):
        y = None
        for a_ref, w_ref in zip(lhs, wos):
            part = jnp.dot(a_ref[rs, :], w_ref[...], preferred_element_type=F32)
            y = part if y is None else y + part
        return y

    def norms(rs, y):
        x1 = x_ref[rs, :] + gt1_ref[0] * _rms(y, gpm_ref[...])
        fx = (_rms(x1, gpf_ref[...]) * (1.0 + sc2_ref[0]) + sh2_ref[0]).astype(BF16)
        return x1, fx

    def ffn(fx):
        acc = None
        for lo in range(0, wg_ref.shape[1], f_chunk):
            gate = jnp.dot(fx, wg_ref[:, lo:lo + f_chunk], preferred_element_type=F32)
            up = jnp.dot(fx, wu_ref[:, lo:lo + f_chunk], preferred_element_type=F32)
            hid = (_silu(gate) * up).astype(BF16)
            part = jnp.dot(hid, wd_ref[lo:lo + f_chunk, :], preferred_element_type=F32)
            acc = part if acc is None else acc + part
        return acc

    ra, rb = slice(0, hm), slice(hm, 2 * hm)
    ya = outproj(ra)
    yb = outproj(rb)
    x1a, fxa = norms(ra, ya)
    fa = ffn(fxa)
    x1b, fxb = norms(rb, yb)
    o_ref[ra, :] = x1a + gt2_ref[0] * _rms(fa, gpo_ref[...])
    fb = ffn(fxb)
    o_ref[rb, :] = x1b + gt2_ref[0] * _rms(fb, gpo_ref[...])


def _tail(lhs, wos, x2, gpm, gt1, gpf, sh2, sc2, wg, wu, wd, gpo, gt2, rows_per_cond, cond_base):
    n, d = x2.shape
    f = wg.shape[1]
    tm = min(TAIL_TILE, n)
    assert f % TAIL_F_CHUNK == 0
    tiles_per_cond = rows_per_cond // tm
    cond_map = lambda i: (cond_base + i // tiles_per_cond, 0, 0)
    cond = pl.BlockSpec((1, 1, d), cond_map)
    vec = _const_spec((1, d))
    return pl.pallas_call(
        functools.partial(_tail_kernel, n_lhs=len(lhs), f_chunk=TAIL_F_CHUNK),
        grid=(n // tm,),
        in_specs=([pl.BlockSpec((tm, a.shape[1]), lambda i: (i, 0)) for a in lhs]
                  + [_const_spec(w.shape) for w in wos]
                  + [pl.BlockSpec((tm, d), lambda i: (i, 0)), vec, cond, vec, cond, cond,
                     _const_spec(wg.shape), _const_spec(wu.shape), _const_spec(wd.shape), vec, cond]),
        out_specs=pl.BlockSpec((tm, d), lambda i: (i, 0)),
        out_shape=jax.ShapeDtypeStruct((n, d), F32),
        compiler_params=_cparams("parallel"),
        name="tail",
    )(*lhs, *wos, x2, gpm.reshape(1, d), gt1, gpf.reshape(1, d), sh2, sc2, wg, wu, wd,
      gpo.reshape(1, d), gt2)


def kernel(x, c, ctx, c_ctx, w_mod, b_mod, g_pre_mix, g_post_mix, g_pre_ffn, g_post_ffn, w_ffn_gate, w_ffn_up, w_ffn_down, w_in_ab, conv_w, conv_b, lru_w_a, lru_b_a, lru_w_i, lru_b_i, lru_lam, w_out_ab, w_qkv_na, rpb_na, w_out_na):
    bsz, seq, d = x.shape
    ctx_len = ctx.shape[1]
    depth = w_mod.shape[0]
    lru_w = conv_w.shape[-1]
    n_cond = 8
    assert bsz < n_cond
    ctx_cond = bsz

    cond = jnp.concatenate([c, c_ctx[None], jnp.zeros((n_cond - bsz - 1, d), F32)], 0)
    mod = _mod_vectors(cond, w_mod, b_mod).reshape(depth, n_cond, 6, 1, d)

    xs = x.reshape(bsz * seq, d)
    cs = ctx.reshape(bsz * ctx_len, d)
    for l in range(depth):
        last = l == depth - 1
        sh1, sc1, gt1, sh2, sc2, gt2 = (mod[l, :, j] for j in range(6))
        wg, wu, wd = (w[l].astype(BF16) for w in (w_ffn_gate, w_ffn_up, w_ffn_down))
        norm_args = (g_post_mix[l], gt1, g_pre_ffn[l], sh2, sc2, wg, wu, wd, g_post_ffn[l], gt2)
        if l % 2 == 0:
            e = l // 2
            w_in = w_in_ab[e].astype(BF16)
            w_out = w_out_ab[e].astype(BF16)
            splits = ((0, lru_w, 1.0), (lru_w, 2 * lru_w, 1.0), (2 * lru_w, w_in.shape[1], 1.0))
            gates = [(_block_diag_halves(lru_w_a[e, dr]), _block_diag_halves(lru_w_i[e, dr]))
                     for dr in range(2)]

            def mixer(tokens, t_len, rows_per_cond, cond_base, h0):
                u, g, f = _inproj(tokens, g_pre_mix[l], sh1, sc1, w_in, splits, (F32, F32, F32),
                                  rows_per_cond, cond_base, slabbed=(2,))
                u, g = (a.reshape(bsz, t_len, -1) for a in (u, g))
                f = f.reshape(f.shape[0], bsz, t_len, LANES)
                lru = functools.partial(_lru_direction, u, cw=conv_w[e], cb=conv_b[e])
                hf, end_f = lru(h0[0], wa_bd=gates[0][0], wi_bd=gates[0][1], ba=lru_b_a[e, 0],
                                bi=lru_b_i[e, 0], lam=lru_lam[e, 0], reverse=False)
                lr, end_b = lru(h0[1], wa_bd=gates[1][0], wi_bd=gates[1][1], ba=lru_b_a[e, 1],
                                bi=lru_b_i[e, 1], lam=lru_lam[e, 1], reverse=True, other=hf, gate=g)
                fo = _fourier(f, f.shape[0] * LANES // FFT_GROUPS)
                lhs = [lr.reshape(tokens.shape[0], -1), fo.reshape(tokens.shape[0], -1)]
                return lhs, (end_f, end_b)

            zeros = jnp.zeros((bsz, 8, lru_w), F32)
            lhs_c, ends = mixer(cs, ctx_len, bsz * ctx_len, ctx_cond, (zeros, zeros))
            lhs_x, _ = mixer(xs, seq, seq, 0, ends)
            wos = [w_out[:lru_w], w_out[lru_w:]]
            if not last:
                cs = _tail(lhs_c, wos, cs, *norm_args, bsz * ctx_len, ctx_cond)
            xs = _tail(lhs_x, wos, xs, *norm_args, seq, 0)
        else:
            o = l // 2
            w_qkv = w_qkv_na[o].astype(BF16)
            n_heads = rpb_na.shape[1]
            scale = LOG2E * (d // n_heads) ** -0.5
            q_split, v_split = (0, d, scale), (2 * d, 3 * d, 1.0)
            w_k_t = w_qkv[:, d:2 * d].T
            q, v, kt = _inproj(xs, g_pre_mix[l], sh1, sc1, w_qkv, (q_split, v_split), (BF16, BF16),
                               seq, 0, wt=w_k_t)
            vc, kct = _inproj(cs, g_pre_mix[l], sh1, sc1, w_qkv, (v_split,), (BF16,),
                              bsz * ctx_len, ctx_cond, wt=w_k_t)
            grid = (bsz, seq // GRID_W, GRID_W, d)
            att = _natten(q.reshape(grid), kt, v.reshape(grid), kct, vc.reshape(bsz, ctx_len, d),
                          _natten_bias_table(rpb_na[o]))
            if not last:
                raise NotImplementedError("context update after an attention layer")
            xs = _tail([att.reshape(bsz * seq, d)], [w_out_na[o].astype(BF16)], xs, *norm_args, seq, 0)
    return xs.reshape(bsz, seq, d)
```

```python
import functools
import math

import numpy as np
import jax
import jax.numpy as jnp
from jax import lax
from jax.experimental import pallas as pl
from jax.experimental.pallas import tpu as pltpu

F32 = jnp.float32
BF16 = jnp.bfloat16

EPS = 1e-6
LRU_C = 8.0
CONV_LEFT = 2
FFT_GROUPS = 8
GRID_W = 64
NA_KH = 8
NA_KW = 16
DFT_RADIX = 8
NEG = -1e30
LOG2E = math.log2(math.e)

VMEM_LIMIT_BYTES = 56 * 1024 * 1024
ROW_TILE = 1024
LRU_TILE = 1024
TAIL_TILE = 1024
TAIL_F_CHUNK = 256
LANES = 128
MXU_DIM = 256


def _cparams(*sem):
    return pltpu.CompilerParams(dimension_semantics=sem, vmem_limit_bytes=VMEM_LIMIT_BYTES)


def _const_spec(shape):
    nd = len(shape)
    return pl.BlockSpec(shape, lambda *_: (0,) * nd, pipeline_mode=pl.Buffered(1))


def _sigmoid(x):
    return 0.5 * (1.0 + jnp.tanh(0.5 * x))


def _silu(x):
    return x * _sigmoid(x)


def _gelu_tanh(x):
    c = math.sqrt(2.0 / math.pi)
    return 0.5 * x * (1.0 + jnp.tanh(c * (x + 0.044715 * (x * x * x))))


def _rms(x, g):
    return x * lax.rsqrt(jnp.mean(x * x, axis=-1, keepdims=True) + EPS) * g


def _mod_kernel(cond_ref, w_ref, b_ref, o_ref):
    act = _silu(cond_ref[...]).astype(BF16)
    o_ref[0] = jnp.dot(act, w_ref[0].astype(BF16), preferred_element_type=F32) + b_ref[0]


def _mod_vectors(cond, w_mod, b_mod):
    depth, d, n = w_mod.shape
    tn = n // 4
    return pl.pallas_call(
        _mod_kernel,
        grid=(depth, n // tn),
        in_specs=[pl.BlockSpec(cond.shape, lambda l, j: (0, 0)),
                  pl.BlockSpec((1, d, tn), lambda l, j: (l, 0, j)),
                  pl.BlockSpec((1, 1, tn), lambda l, j: (l, 0, j))],
        out_specs=pl.BlockSpec((1, cond.shape[0], tn), lambda l, j: (l, 0, j)),
        out_shape=jax.ShapeDtypeStruct((depth, cond.shape[0], n), F32),
        compiler_params=_cparams("parallel", "parallel"),
        name="mod",
    )(cond, w_mod, b_mod.reshape(depth, 1, n))


def _inproj_kernel(x_ref, g_ref, sh_ref, sc_ref, w_ref, *rest, splits, transposed):
    o_refs = rest
    if transposed:
        wt_ref, o_refs, ot_ref = rest[0], rest[1:-1], rest[-1]
    hm = x_ref.shape[0] // 2

    def norm(rs):
        h = _rms(x_ref[rs, :], g_ref[...]) * (1.0 + sc_ref[0]) + sh_ref[0]
        return h.astype(BF16)

    def project(rs, hb):
        if transposed:
            ot_ref[:, rs] = lax.dot_general(wt_ref[...], hb, (((1,), (1,)), ((), ())),
                                            preferred_element_type=F32).astype(ot_ref.dtype)
        for o_ref, (lo, hi, scale) in zip(o_refs, splits):
            z = jnp.dot(hb, w_ref[:, lo:hi], preferred_element_type=F32)
            if scale != 1.0:
                z = z * scale
            if len(o_ref.shape) == 3:
                for j in range(o_ref.shape[0]):
                    o_ref[j, rs, :] = z[:, j * LANES:(j + 1) * LANES].astype(o_ref.dtype)
            else:
                o_ref[rs, :] = z.astype(o_ref.dtype)

    ra, rb = slice(0, hm), slice(hm, 2 * hm)
    ha = norm(ra)
    hb = norm(rb)
    project(ra, ha)
    project(rb, hb)


def _inproj(x2, g, sh, sc, w, splits, dtypes, rows_per_cond, cond_base, slabbed=(), wt=None):
    n, d = x2.shape
    tm = min(ROW_TILE, n)
    tiles_per_cond = rows_per_cond // tm
    cond_map = lambda i: (cond_base + i // tiles_per_cond, 0, 0)
    out_specs, out_shape = [], []
    for j, ((lo, hi, _), dt) in enumerate(zip(splits, dtypes)):
        if j in slabbed:
            ns = (hi - lo) // LANES
            out_specs.append(pl.BlockSpec((ns, tm, LANES), lambda i: (0, i, 0)))
            out_shape.append(jax.ShapeDtypeStruct((ns, n, LANES), dt))
        else:
            out_specs.append(pl.BlockSpec((tm, hi - lo), lambda i: (i, 0)))
            out_shape.append(jax.ShapeDtypeStruct((n, hi - lo), dt))
    in_specs = [pl.BlockSpec((tm, d), lambda i: (i, 0)),
                _const_spec((1, d)),
                pl.BlockSpec((1, 1, d), cond_map),
                pl.BlockSpec((1, 1, d), cond_map),
                _const_spec(w.shape)]
    args = [x2, g.reshape(1, d), sh, sc, w]
    if wt is not None:
        in_specs.append(_const_spec(wt.shape))
        args.append(wt)
        out_specs.append(pl.BlockSpec((wt.shape[0], tm), lambda i: (0, i)))
        out_shape.append(jax.ShapeDtypeStruct((wt.shape[0], n), BF16))
    return pl.pallas_call(
        functools.partial(_inproj_kernel, splits=splits, transposed=wt is not None),
        grid=(n // tm,),
        in_specs=in_specs,
        out_specs=out_specs,
        out_shape=out_shape,
        compiler_params=_cparams("parallel"),
        name="inproj",
    )(*args)


def _lru_kernel(*refs, reverse, combine, tt, nt):
    if combine:
        (up_ref, u_ref, un_ref, h0_ref, cw_ref, cb_ref, wa_ref, wi_ref, ba_ref, bi_ref, lam_ref,
         ro_ref, g_ref, out_ref, hl_ref, ext_s, a_s, b_s, h_s) = refs
    else:
        (up_ref, u_ref, un_ref, h0_ref, cw_ref, cb_ref, wa_ref, wi_ref, ba_ref, bi_ref, lam_ref,
         out_ref, hl_ref, ext_s, a_s, b_s, h_s) = refs
    step = pl.program_id(1)
    tile = nt - 1 - step if reverse else step
    c = u_ref.shape[-1]
    half = c // 2

    @pl.when(step == 0)
    def _():
        h_s[...] = h0_ref[0, 0:1, :]

    ext_s[0:8, :] = jnp.where(tile > 0, up_ref[0], 0.0)
    ext_s[8:8 + tt, :] = u_ref[0]
    ext_s[8 + tt:16 + tt, :] = jnp.where(tile < nt - 1, un_ref[0], 0.0)
    uc = cb_ref[...]
    for k in range(cw_ref.shape[0]):
        off = 8 - CONV_LEFT + k
        uc = uc + ext_s[off:off + tt, :] * cw_ref[k:k + 1, :]

    ub = uc.astype(BF16)
    lam = lam_ref[...]
    softplus_neg_lam = jnp.maximum(-lam, 0.0) + jnp.log1p(jnp.exp(-jnp.abs(lam)))
    coef = (-0.5 * LRU_C) * softplus_neg_lam
    for hb in range(2):
        sl = slice(hb * half, (hb + 1) * half)
        tr = jnp.tanh(jnp.dot(ub[:, sl], wa_ref[hb], preferred_element_type=F32) + ba_ref[:, sl])
        ti = jnp.tanh(jnp.dot(ub[:, sl], wi_ref[hb], preferred_element_type=F32) + bi_ref[:, sl])
        log_a = coef[:, sl] * tr + coef[:, sl]
        a = jnp.exp(log_a)
        a_s[:, sl] = a
        uh = 0.5 * uc[:, sl]
        b_s[:, sl] = jnp.sqrt(-jnp.tanh(log_a) * (a * a + 1.0)) * (uh * ti + uh)

    row = lax.broadcasted_iota(jnp.int32, (8, c), 0)
    ng = tt // 8

    def group(jj, h):
        j = ng - 1 - jj if reverse else jj
        off = pl.multiple_of(j * 8, 8)
        a = a_s[pl.ds(off, 8), :]
        b = b_s[pl.ds(off, 8), :]
        for k in (1, 2, 4):
            shift = 8 - k if reverse else k
            keep = (row < 8 - k) if reverse else (row >= k)
            a_sh = pltpu.roll(a, shift, 0)
            b_sh = pltpu.roll(b, shift, 0)
            b = jnp.where(keep, a * b_sh + b, b)
            a = jnp.where(keep, a * a_sh, a)
        hj = b + a * h
        b_s[pl.ds(off, 8), :] = hj
        return hj[0:1, :] if reverse else hj[7:8, :]

    h_last = lax.fori_loop(0, ng, group, h_s[...], unroll=8)
    h_s[...] = h_last
    hl_ref[0] = jnp.broadcast_to(h_last, hl_ref.shape[1:])
    if combine:
        out_ref[0] = ((ro_ref[0] + b_s[...]) * _gelu_tanh(g_ref[0])).astype(out_ref.dtype)
    else:
        out_ref[0] = b_s[...]


def _lru_direction(u, h0, cw, cb, wa_bd, wi_bd, ba, bi, lam, reverse, other=None, gate=None):
    bsz, t, c = u.shape
    tt = min(LRU_TILE, t)
    nt = t // tt
    g8 = tt // 8
    combine = other is not None
    tile = (lambda s: nt - 1 - s) if reverse else (lambda s: s)
    cur = pl.BlockSpec((1, tt, c), lambda b, s: (b, tile(s), 0))
    in_specs = [pl.BlockSpec((1, 8, c), lambda b, s: (b, jnp.maximum(tile(s) * g8 - 1, 0), 0)),
                cur,
                pl.BlockSpec((1, 8, c), lambda b, s: (b, jnp.minimum((tile(s) + 1) * g8, t // 8 - 1), 0)),
                pl.BlockSpec((1, 8, c), lambda b, s: (b, 0, 0)),
                _const_spec(cw.shape), _const_spec((1, c)),
                _const_spec(wa_bd.shape), _const_spec(wi_bd.shape),
                _const_spec((1, c)), _const_spec((1, c)), _const_spec((1, c))]
    args = [u, u, u, h0, cw, cb.reshape(1, c), wa_bd, wi_bd, ba.reshape(1, c), bi.reshape(1, c),
            lam.reshape(1, c)]
    if combine:
        in_specs += [cur, cur]
        args += [other, gate]
    return pl.pallas_call(
        functools.partial(_lru_kernel, reverse=reverse, combine=combine, tt=tt, nt=nt),
        grid=(bsz, nt),
        in_specs=in_specs,
        out_specs=[cur, pl.BlockSpec((1, 8, c), lambda b, s: (b, 0, 0))],
        out_shape=[jax.ShapeDtypeStruct((bsz, t, c), BF16 if combine else F32),
                   jax.ShapeDtypeStruct((bsz, 8, c), F32)],
        scratch_shapes=[pltpu.VMEM((tt + 16, c), F32), pltpu.VMEM((tt, c), F32),
                        pltpu.VMEM((tt, c), F32), pltpu.VMEM((1, c), F32)],
        compiler_params=_cparams("parallel", "arbitrary"),
        name="lru_bwd" if reverse else "lru_fwd",
    )(*args)


def _block_diag_halves(w):
    h, d, _ = w.shape
    per = h // 2
    eye = jnp.eye(per, dtype=w.dtype)
    wh = w.reshape(2, per, d, d)
    bd = jnp.einsum("pq,cpij->cpiqj", eye, wh).reshape(2, per * d, per * d)
    return bd.astype(BF16)


def _fourier_kernel(x_ref, cs_ref, tc_ref, ts_ref, cc_ref, sc_ref, out_ref, y_s, *, n2):
    radix, _, cw = y_s.shape
    for n1 in range(radix):
        x = jnp.concatenate([x_ref[j, 0, pl.ds(n1, n2, stride=radix), :] for j in range(x_ref.shape[0])],
                            axis=-1)
        y_s[n1] = jnp.dot(cs_ref[...], x.astype(BF16), preferred_element_type=F32)

    def group(g, carry):
        r = pl.multiple_of(g * 8, 8)
        for half in range(cw // LANES):
            ls = slice(half * LANES, (half + 1) * LANES)
            zr, zi = [], []
            for n1 in range(radix):
                yc = y_s[n1, pl.ds(r, 8), ls]
                ys = y_s[n1, pl.ds(n2 + r, 8), ls]
                if n1 == 0:
                    zr.append(yc)
                    zi.append(-ys)
                else:
                    tc = tc_ref[n1, pl.ds(r, 8), :]
                    ts = ts_ref[n1, pl.ds(r, 8), :]
                    zr.append(yc * tc - ys * ts)
                    zi.append(-(yc * ts + ys * tc))
            ar, ai = _dft8(zr, zi)
            for k1 in range(radix):
                y_s[k1, pl.ds(r, 8), ls] = ar[k1]
                y_s[k1, pl.ds(n2 + r, 8), ls] = ai[k1]
        return carry

    lax.fori_loop(0, n2 // 8, group, 0)

    for k1 in range(radix):
        o = (jnp.dot(y_s[k1, 0:n2, :].astype(BF16), cc_ref[...], preferred_element_type=F32)
             + jnp.dot(y_s[k1, n2:2 * n2, :].astype(BF16), sc_ref[...], preferred_element_type=F32))
        out_ref[0, k1 * n2:(k1 + 1) * n2, :] = o.astype(out_ref.dtype)


def _dft8(zr, zi):
    h = math.sqrt(0.5)

    def dft4(r, i):
        s0r, s0i = r[0] + r[2], i[0] + i[2]
        s1r, s1i = r[0] - r[2], i[0] - i[2]
        s2r, s2i = r[1] + r[3], i[1] + i[3]
        s3r, s3i = r[1] - r[3], i[1] - i[3]
        return ([s0r + s2r, s1r + s3i, s0r - s2r, s1r - s3i],
                [s0i + s2i, s1i - s3r, s0i - s2i, s1i + s3r])

    er, ei = dft4(zr[0::2], zi[0::2])
    odr, odi = dft4(zr[1::2], zi[1::2])
    tr = [odr[0], h * (odr[1] + odi[1]), odi[2], h * (odi[3] - odr[3])]
    ti = [odi[0], h * (odi[1] - odr[1]), -odr[2], -h * (odr[3] + odi[3])]
    return ([er[k] + tr[k] for k in range(4)] + [er[k] - tr[k] for k in range(4)],
            [ei[k] + ti[k] for k in range(4)] + [ei[k] - ti[k] for k in range(4)])


def _fourier(f, group_dim):
    ns, bsz, t, _ = f.shape
    c = ns * LANES
    radix = DFT_RADIX
    n2 = t // radix
    cw = MXU_DIM
    nh = c // cw
    spb = cw // LANES
    k = np.arange(n2)
    ang = 2.0 * np.pi * ((k[:, None] * k[None, :]) % n2) / n2
    cs = np.concatenate([np.cos(ang), np.sin(ang)], 0) / math.sqrt(t)
    n1 = np.arange(radix)
    ang_t = 2.0 * np.pi * ((n1[:, None] * k[None, :]) % t) / t
    tw_shape = (radix, n2, LANES)
    j = np.arange(cw)
    same = (j[:, None] // group_dim) == (j[None, :] // group_dim)
    ang_c = 2.0 * np.pi * ((j[:, None] * j[None, :]) % group_dim) / group_dim
    ccm = np.where(same, np.cos(ang_c), 0.0) / math.sqrt(group_dim)
    scm = np.where(same, np.sin(ang_c), 0.0) / math.sqrt(group_dim)
    assert radix == 8, "_dft8 is the radix stage"
    return pl.pallas_call(
        functools.partial(_fourier_kernel, n2=n2),
        grid=(bsz, nh),
        in_specs=[pl.BlockSpec((spb, 1, t, LANES), lambda b, h: (h, b, 0, 0)),
                  _const_spec((2 * n2, n2)),
                  _const_spec(tw_shape), _const_spec(tw_shape),
                  _const_spec((cw, cw)), _const_spec((cw, cw))],
        out_specs=pl.BlockSpec((1, t, cw), lambda b, h: (b, 0, h), pipeline_mode=pl.Buffered(1)),
        out_shape=jax.ShapeDtypeStruct((bsz, t, c), BF16),
        scratch_shapes=[pltpu.VMEM((radix, 2 * n2, cw), F32)],
        compiler_params=_cparams("parallel", "parallel"),
        name="fourier",
    )(f,
      jnp.asarray(cs, F32).astype(BF16),
      jnp.broadcast_to(jnp.asarray(np.cos(ang_t), F32)[:, :, None], tw_shape),
      jnp.broadcast_to(jnp.asarray(np.sin(ang_t), F32)[:, :, None], tw_shape),
      jnp.asarray(ccm, F32).astype(BF16), jnp.asarray(scm, F32).astype(BF16))


def _natten_kernel(q_ref, kt_ref, v_ref, kct_ref, vc_ref, tz_ref, o_ref, bias_s, *, qr_n, kh):
    rows, w, lanes = q_ref.shape[1:]
    nrb = rows // qr_n
    hd = lanes // 2
    half_n = qr_n // 2
    band = half_n + kh
    nq = half_n * w
    lane = lax.broadcasted_iota(jnp.int32, (1, lanes), 1)
    first = lane < hd
    ones = jnp.ones((1, lanes), BF16)
    mask = [(lane // hd == hh).astype(F32).astype(BF16) for hh in range(2)]
    vc = vc_ref[0]
    vc1 = jnp.concatenate([vc, jnp.broadcast_to(ones, vc.shape)], axis=1)

    def band_start(r_first):
        return jnp.clip(r_first - kh // 2, 0, rows - band)

    def rebuild_bias(rb):
        for half in range(2):
            r_first = rb * qr_n + half * half_n
            kr0 = band_start(r_first)
            for qr in range(half_n):
                r = r_first + qr
                rs = jnp.clip(r - kh // 2, 0, rows - kh)
                for m in range(band // 2):
                    kr = kr0 + 2 * m
                    ae = kr - r + (NA_KH - 1) + NA_KH
                    kr_lane = kr + lane // hd
                    ok = (kr_lane >= rs) & (kr_lane < rs + kh)
                    for hh in range(2):
                        row0 = hh * nq + qr * w
                        bias_s[half, row0:row0 + w, m * lanes:(m + 1) * lanes] = jnp.where(
                            ok, tz_ref[hh, ae], NEG)

    def scores(rb, half):
        r_first = rb * qr_n + half * half_n
        kr0 = band_start(r_first)
        q = q_ref[0, pl.ds(r_first, half_n)].reshape(nq, lanes)
        qq = jnp.concatenate([q * mask[0], q * mask[1]], axis=0)
        kt = kt_ref[:, pl.ds(pl.multiple_of(kr0 * w, MXU_DIM), band * w)]
        s_loc = jnp.dot(qq, kt, preferred_element_type=F32) + bias_s[half]
        s_ctx = jnp.dot(qq, kct_ref[...], preferred_element_type=F32)
        return s_loc, s_ctx

    def softmax(s):
        s_loc, s_ctx = s
        mx = jnp.maximum(jnp.max(s_loc, axis=-1, keepdims=True), jnp.max(s_ctx, axis=-1, keepdims=True))
        return jnp.exp2((s_loc - mx).astype(BF16)), jnp.exp2((s_ctx - mx).astype(BF16))

    def attend(rb, half, p):
        r_first = rb * qr_n + half * half_n
        vb = v_ref[0, pl.ds(band_start(r_first), band)].reshape(band * w, lanes)
        vb1 = jnp.concatenate([vb, jnp.broadcast_to(ones, vb.shape)], axis=1)
        acc = (jnp.dot(p[0], vb1, preferred_element_type=F32)
               + jnp.dot(p[1], vc1, preferred_element_type=F32))
        o0 = acc[:nq, :lanes] / acc[:nq, lanes:]
        o1 = acc[nq:, :lanes] / acc[nq:, lanes:]
        o_ref[0, pl.ds(r_first, half_n)] = jnp.where(first, o0, o1).reshape(
            half_n, w, lanes).astype(o_ref.dtype)

    def row_block(rb, carry):
        pl.when((rb <= 1) | (rb == nrb - 1))(lambda: rebuild_bias(rb))
        s_a = scores(rb, 0)
        s_b = scores(rb, 1)
        p_a = softmax(s_a)
        attend(rb, 0, p_a)
        p_b = softmax(s_b)
        attend(rb, 1, p_b)
        return carry

    lax.fori_loop(0, nrb, row_block, 0)


def _natten_bias_table(rpb):
    h = rpb.shape[0]
    w = GRID_W
    qc = np.arange(w)
    cs = np.clip(qc - NA_KW // 2, 0, w - NA_KW)
    kc = np.arange(w)
    inside = (kc[None, :] >= cs[:, None]) & (kc[None, :] < cs[:, None] + NA_KW)
    dc = kc[None, :] - qc[:, None] + NA_KW - 1
    pick = (np.arange(2 * NA_KW - 1)[:, None, None] == dc[None]) & inside[None]
    slabs = jnp.einsum("had,dqk->haqk", LOG2E * rpb, jnp.asarray(pick, F32),
                       precision=lax.Precision.HIGHEST)
    slabs = slabs + jnp.asarray(np.where(inside, 0.0, NEG), F32)
    n_slab = 4 * NA_KH + 1
    pad_lo = NA_KH
    pad_hi = n_slab - pad_lo - slabs.shape[1]
    slabs = jnp.pad(slabs, ((0, 0), (pad_lo, pad_hi), (0, 0), (0, 0)), constant_values=NEG)
    return jnp.concatenate([slabs[:, :-1], slabs[:, 1:]], axis=-1)


def _natten(q, kt, v, kct, vc, table):
    bsz, rows, w, d = q.shape
    l = vc.shape[1]
    qr_n = NA_KH
    kh = min(NA_KH, rows)
    band = qr_n // 2 + kh
    assert w == GRID_W and 2 * w == LANES and rows >= 2 * qr_n and rows % qr_n == 0
    assert (qr_n // 2 * w) % MXU_DIM == 0 and l % LANES == 0
    nhp = d // LANES
    image = pl.BlockSpec((1, rows, w, LANES), lambda hp, b: (b, 0, 0, hp))
    return pl.pallas_call(
        functools.partial(_natten_kernel, qr_n=qr_n, kh=kh),
        grid=(nhp, bsz),
        in_specs=[image,
                  pl.BlockSpec((LANES, rows * w), lambda hp, b: (hp, b)),
                  image,
                  pl.BlockSpec((LANES, l), lambda hp, b: (hp, b)),
                  pl.BlockSpec((1, l, LANES), lambda hp, b: (b, 0, hp)),
                  pl.BlockSpec((2,) + table.shape[1:], lambda hp, b: (hp, 0, 0, 0))],
        out_specs=image,
        out_shape=jax.ShapeDtypeStruct(q.shape, BF16),
        scratch_shapes=[pltpu.VMEM((2, qr_n * w, band * w), F32)],
        compiler_params=_cparams("parallel", "parallel"),
        name="natten",
    )(q, kt, v, kct, vc, table)


def _tail_kernel(*refs, n_lhs, f_chunk):
    lhs = refs[:n_lhs]
    (wo_ref, x_ref, gpm_ref, gt1_ref, gpf_ref, sh2_ref, sc2_ref, wg_ref, wu_ref, wd_ref, gpo_ref,
     gt2_ref, o_ref) = refs[n_lhs:]
    hm = x_ref.shape[0] // 2

    def outproj(rs):
        y, k0 = None, 0
        for a_ref in lhs:
            k1 = k0 + a_ref.shape[1]
            part = jnp.dot(a_ref[rs, :], wo_ref[k0:k1, :], preferred_element_type=F32)
            y = part if y is None else y + part
            k0 = k1
        return y

    def norms(rs, y):
        x1 = x_ref[rs, :] + gt1_ref[0] * _rms(y, gpm_ref[...])
        fx = (_rms(x1, gpf_ref[...]) * (1.0 + sc2_ref[0]) + sh2_ref[0]).astype(BF16)
        return x1, fx

    def ffn(fx):
        acc = None
        for lo in range(0, wg_ref.shape[1], f_chunk):
            gate = jnp.dot(fx, wg_ref[:, lo:lo + f_chunk], preferred_element_type=F32)
            up = jnp.dot(fx, wu_ref[:, lo:lo + f_chunk], preferred_element_type=F32)
            hid = (_silu(gate) * up).astype(BF16)
            part = jnp.dot(hid, wd_ref[lo:lo + f_chunk, :], preferred_element_type=F32)
            acc = part if acc is None else acc + part
        return acc

    ra, rb = slice(0, hm), slice(hm, 2 * hm)
    ya = outproj(ra)
    yb = outproj(rb)
    x1a, fxa = norms(ra, ya)
    fa = ffn(fxa)
    x1b, fxb = norms(rb, yb)
    o_ref[ra, :] = x1a + gt2_ref[0] * _rms(fa, gpo_ref[...])
    fb = ffn(fxb)
    o_ref[rb, :] = x1b + gt2_ref[0] * _rms(fb, gpo_ref[...])


def _tail(lhs, wo, x2, gpm, gt1, gpf, sh2, sc2, wg, wu, wd, gpo, gt2, rows_per_cond, cond_base):
    n, d = x2.shape
    f = wg.shape[1]
    tm = min(TAIL_TILE, n)
    assert f % TAIL_F_CHUNK == 0
    tiles_per_cond = rows_per_cond // tm
    cond_map = lambda i: (cond_base + i // tiles_per_cond, 0, 0)
    cond = pl.BlockSpec((1, 1, d), cond_map)
    vec = _const_spec((1, d))
    return pl.pallas_call(
        functools.partial(_tail_kernel, n_lhs=len(lhs), f_chunk=TAIL_F_CHUNK),
        grid=(n // tm,),
        in_specs=([pl.BlockSpec((tm, a.shape[1]), lambda i: (i, 0)) for a in lhs]
                  + [_const_spec(wo.shape),
                     pl.BlockSpec((tm, d), lambda i: (i, 0)), vec, cond, vec, cond, cond,
                     _const_spec(wg.shape), _const_spec(wu.shape), _const_spec(wd.shape), vec, cond]),
        out_specs=pl.BlockSpec((tm, d), lambda i: (i, 0)),
        out_shape=jax.ShapeDtypeStruct((n, d), F32),
        compiler_params=_cparams("parallel"),
        name="tail",
    )(*lhs, wo, x2, gpm.reshape(1, d), gt1, gpf.reshape(1, d), sh2, sc2, wg, wu, wd,
      gpo.reshape(1, d), gt2)


def kernel(x, c, ctx, c_ctx, w_mod, b_mod, g_pre_mix, g_post_mix, g_pre_ffn, g_post_ffn, w_ffn_gate, w_ffn_up, w_ffn_down, w_in_ab, conv_w, conv_b, lru_w_a, lru_b_a, lru_w_i, lru_b_i, lru_lam, w_out_ab, w_qkv_na, rpb_na, w_out_na):
    bsz, seq, d = x.shape
    ctx_len = ctx.shape[1]
    depth = w_mod.shape[0]
    lru_w = conv_w.shape[-1]
    n_cond = 8
    assert bsz < n_cond
    ctx_cond = bsz

    cond = jnp.concatenate([c, c_ctx[None], jnp.zeros((n_cond - bsz - 1, d), F32)], 0)
    mod = _mod_vectors(cond, w_mod, b_mod).reshape(depth, n_cond, 6, 1, d)

    xs = x.reshape(bsz * seq, d)
    cs = ctx.reshape(bsz * ctx_len, d)
    for l in range(depth):
        last = l == depth - 1
        sh1, sc1, gt1, sh2, sc2, gt2 = (mod[l, :, j] for j in range(6))
        wg, wu, wd = (w[l].astype(BF16) for w in (w_ffn_gate, w_ffn_up, w_ffn_down))
        norm_args = (g_post_mix[l], gt1, g_pre_ffn[l], sh2, sc2, wg, wu, wd, g_post_ffn[l], gt2)
        if l % 2 == 0:
            e = l // 2
            w_in = w_in_ab[e].astype(BF16)
            w_out = w_out_ab[e].astype(BF16)
            splits = ((0, lru_w, 1.0), (lru_w, 2 * lru_w, 1.0), (2 * lru_w, w_in.shape[1], 1.0))
            gates = [(_block_diag_halves(0.5 * lru_w_a[e, dr]), _block_diag_halves(0.5 * lru_w_i[e, dr]))
                     for dr in range(2)]

            def mixer(tokens, t_len, rows_per_cond, cond_base, h0):
                u, g, f = _inproj(tokens, g_pre_mix[l], sh1, sc1, w_in, splits, (F32, F32, F32),
                                  rows_per_cond, cond_base, slabbed=(2,))
                u, g = (a.reshape(bsz, t_len, -1) for a in (u, g))
                f = f.reshape(f.shape[0], bsz, t_len, LANES)
                lru = functools.partial(_lru_direction, u, cw=conv_w[e], cb=conv_b[e])
                hf, end_f = lru(h0[0], wa_bd=gates[0][0], wi_bd=gates[0][1], ba=0.5 * lru_b_a[e, 0],
                                bi=0.5 * lru_b_i[e, 0], lam=lru_lam[e, 0], reverse=False)
                lr, end_b = lru(h0[1], wa_bd=gates[1][0], wi_bd=gates[1][1], ba=0.5 * lru_b_a[e, 1],
                                bi=0.5 * lru_b_i[e, 1], lam=lru_lam[e, 1], reverse=True, other=hf, gate=g)
                fo = _fourier(f, f.shape[0] * LANES // FFT_GROUPS)
                lhs = [lr.reshape(tokens.shape[0], -1), fo.reshape(tokens.shape[0], -1)]
                return lhs, (end_f, end_b)

            zeros = jnp.zeros((bsz, 8, lru_w), F32)
            lhs_c, ends = mixer(cs, ctx_len, bsz * ctx_len, ctx_cond, (zeros, zeros))
            lhs_x, _ = mixer(xs, seq, seq, 0, ends)
            if not last:
                cs = _tail(lhs_c, w_out, cs, *norm_args, bsz * ctx_len, ctx_cond)
            xs = _tail(lhs_x, w_out, xs, *norm_args, seq, 0)
        else:
            o = l // 2
            w_qkv = w_qkv_na[o].astype(BF16)
            n_heads = rpb_na.shape[1]
            scale = LOG2E * (d // n_heads) ** -0.5
            q_split, v_split = (0, d, scale), (2 * d, 3 * d, 1.0)
            w_k_t = w_qkv_na[o][:, d:2 * d].T.astype(BF16)
            q, v, kt = _inproj(xs, g_pre_mix[l], sh1, sc1, w_qkv, (q_split, v_split), (BF16, BF16),
                               seq, 0, wt=w_k_t)
            vc, kct = _inproj(cs, g_pre_mix[l], sh1, sc1, w_qkv, (v_split,), (BF16,),
                              bsz * ctx_len, ctx_cond, wt=w_k_t)
            grid = (bsz, seq // GRID_W, GRID_W, d)
            att = _natten(q.reshape(grid), kt, v.reshape(grid), kct, vc.reshape(bsz, ctx_len, d),
                          _natten_bias_table(rpb_na[o]))
            if not last:
                raise NotImplementedError("context update after an attention layer")
            xs = _tail([att.reshape(bsz * seq, d)], w_out_na[o].astype(BF16), xs, *norm_args, seq, 0)
    return xs.reshape(bsz, seq, d)
```

```python
import functools
import math

import numpy as np
import jax
import jax.numpy as jnp
from jax import lax
from jax.experimental import pallas as pl
from jax.experimental.pallas import tpu as pltpu

F32 = jnp.float32
BF16 = jnp.bfloat16

EPS = 1e-6
LRU_C = 8.0
CONV_LEFT = 2
FFT_GROUPS = 8
GRID_W = 64
NA_KH = 8
NA_KW = 16
DFT_RADIX = 8
NEG = -1e30
LOG2E = math.log2(math.e)

VMEM_LIMIT_BYTES = 56 * 1024 * 1024
ROW_TILE = 1024
LRU_TILE = 512
LRU_BATCH = 2
TAIL_TILE = 1024
TAIL_F_CHUNK = 256
LANES = 128
MXU_DIM = 256


def _cparams(*sem):
    return pltpu.CompilerParams(dimension_semantics=sem, vmem_limit_bytes=VMEM_LIMIT_BYTES)


def _const_spec(shape):
    nd = len(shape)
    return pl.BlockSpec(shape, lambda *_: (0,) * nd, pipeline_mode=pl.Buffered(1))


def _sigmoid(x):
    return 0.5 * (1.0 + jnp.tanh(0.5 * x))


def _silu(x):
    return x * _sigmoid(x)


def _gelu_tanh(x):
    c = math.sqrt(2.0 / math.pi)
    return 0.5 * x * (1.0 + jnp.tanh(c * (x + 0.044715 * (x * x * x))))


def _rms(x, g):
    return x * lax.rsqrt(jnp.mean(x * x, axis=-1, keepdims=True) + EPS) * g


def _mod_kernel(cond_ref, w_ref, b_ref, o_ref):
    act = _silu(cond_ref[...]).astype(BF16)
    o_ref[0] = jnp.dot(act, w_ref[0].astype(BF16), preferred_element_type=F32) + b_ref[0]


def _mod_vectors(cond, w_mod, b_mod):
    depth, d, n = w_mod.shape
    tn = n // 4
    return pl.pallas_call(
        _mod_kernel,
        grid=(depth, n // tn),
        in_specs=[pl.BlockSpec(cond.shape, lambda l, j: (0, 0)),
                  pl.BlockSpec((1, d, tn), lambda l, j: (l, 0, j)),
                  pl.BlockSpec((1, 1, tn), lambda l, j: (l, 0, j))],
        out_specs=pl.BlockSpec((1, cond.shape[0], tn), lambda l, j: (l, 0, j)),
        out_shape=jax.ShapeDtypeStruct((depth, cond.shape[0], n), F32),
        compiler_params=_cparams("parallel", "parallel"),
        name="mod",
    )(cond, w_mod, b_mod.reshape(depth, 1, n))


def _inproj_kernel(x_ref, g_ref, sh_ref, sc_ref, w_ref, *rest, splits, transposed):
    o_refs = rest
    if transposed:
        wt_ref, o_refs, ot_ref = rest[0], rest[1:-1], rest[-1]
    hm = x_ref.shape[0] // 2

    def norm(rs):
        h = _rms(x_ref[rs, :], g_ref[...]) * (1.0 + sc_ref[0]) + sh_ref[0]
        return h.astype(BF16)

    def project(rs, hb):
        if transposed:
            ot_ref[:, rs] = lax.dot_general(wt_ref[...], hb, (((1,), (1,)), ((), ())),
                                            preferred_element_type=F32).astype(ot_ref.dtype)
        for o_ref, (lo, hi, scale) in zip(o_refs, splits):
            z = jnp.dot(hb, w_ref[:, lo:hi], preferred_element_type=F32)
            if scale != 1.0:
                z = z * scale
            if len(o_ref.shape) == 3:
                for j in range(o_ref.shape[0]):
                    o_ref[j, rs, :] = z[:, j * LANES:(j + 1) * LANES].astype(o_ref.dtype)
            else:
                o_ref[rs, :] = z.astype(o_ref.dtype)

    ra, rb = slice(0, hm), slice(hm, 2 * hm)
    ha = norm(ra)
    hb = norm(rb)
    project(ra, ha)
    project(rb, hb)


def _inproj(x2, g, sh, sc, w, splits, dtypes, rows_per_cond, cond_base, slabbed=(), wt=None):
    n, d = x2.shape
    tm = min(ROW_TILE, n)
    tiles_per_cond = rows_per_cond // tm
    cond_map = lambda i: (cond_base + i // tiles_per_cond, 0, 0)
    out_specs, out_shape = [], []
    for j, ((lo, hi, _), dt) in enumerate(zip(splits, dtypes)):
        if j in slabbed:
            ns = (hi - lo) // LANES
            out_specs.append(pl.BlockSpec((ns, tm, LANES), lambda i: (0, i, 0)))
            out_shape.append(jax.ShapeDtypeStruct((ns, n, LANES), dt))
        else:
            out_specs.append(pl.BlockSpec((tm, hi - lo), lambda i: (i, 0)))
            out_shape.append(jax.ShapeDtypeStruct((n, hi - lo), dt))
    in_specs = [pl.BlockSpec((tm, d), lambda i: (i, 0)),
                _const_spec((1, d)),
                pl.BlockSpec((1, 1, d), cond_map),
                pl.BlockSpec((1, 1, d), cond_map),
                _const_spec(w.shape)]
    args = [x2, g.reshape(1, d), sh, sc, w]
    if wt is not None:
        in_specs.append(_const_spec(wt.shape))
        args.append(wt)
        out_specs.append(pl.BlockSpec((wt.shape[0], tm), lambda i: (0, i)))
        out_shape.append(jax.ShapeDtypeStruct((wt.shape[0], n), BF16))
    return pl.pallas_call(
        functools.partial(_inproj_kernel, splits=splits, transposed=wt is not None),
        grid=(n // tm,),
        in_specs=in_specs,
        out_specs=out_specs,
        out_shape=out_shape,
        compiler_params=_cparams("parallel"),
        name="inproj",
    )(*args)


def _lru_kernel(*refs, reverse, combine, tt, nt):
    if combine:
        (up_ref, u_ref, un_ref, h0_ref, cw_ref, cb_ref, wa_ref, wi_ref, ba_ref, bi_ref, lam_ref,
         ro_ref, g_ref, out_ref, hl_ref, ext_s, a_s, b_s, h_s) = refs
    else:
        (up_ref, u_ref, un_ref, h0_ref, cw_ref, cb_ref, wa_ref, wi_ref, ba_ref, bi_ref, lam_ref,
         out_ref, hl_ref, ext_s, a_s, b_s, h_s) = refs
    step = pl.program_id(1)
    tile = nt - 1 - step if reverse else step
    nb, _, c = u_ref.shape
    half = c // 2

    @pl.when(step == 0)
    def _():
        for bi in range(nb):
            h_s[bi] = h0_ref[bi, 0:1, :]

    lam = lam_ref[...]
    softplus_neg_lam = jnp.maximum(-lam, 0.0) + jnp.log1p(jnp.exp(-jnp.abs(lam)))
    coef = (-0.5 * LRU_C) * softplus_neg_lam
    for bi in range(nb):
        ext_s[bi, 0:8, :] = jnp.where(tile > 0, up_ref[bi], 0.0)
        ext_s[bi, 8:8 + tt, :] = u_ref[bi]
        ext_s[bi, 8 + tt:16 + tt, :] = jnp.where(tile < nt - 1, un_ref[bi], 0.0)
        ext = ext_s[bi]
        uc = cb_ref[...]
        for k in range(cw_ref.shape[0]):
            back = (CONV_LEFT - k) % (tt + 16)
            tap = ext if back == 0 else pltpu.roll(ext, back, 0)
            uc = uc + tap[8:8 + tt, :] * cw_ref[k:k + 1, :]

        ub = uc.astype(BF16)
        for hb in range(2):
            sl = slice(hb * half, (hb + 1) * half)
            tr = jnp.tanh(jnp.dot(ub[:, sl], wa_ref[hb], preferred_element_type=F32) + ba_ref[:, sl])
            ti = jnp.tanh(jnp.dot(ub[:, sl], wi_ref[hb], preferred_element_type=F32) + bi_ref[:, sl])
            log_a = coef[:, sl] * tr + coef[:, sl]
            a = jnp.exp(log_a)
            a_s[bi, :, sl] = a
            uh = 0.5 * uc[:, sl]
            w = -jnp.tanh(log_a) * (a * a + 1.0)
            root = jnp.where(w > 0.0, w * lax.rsqrt(w), 0.0)
            b_s[bi, :, sl] = root * (uh * ti + uh)

    row = lax.broadcasted_iota(jnp.int32, (8, c), 0)
    ng = tt // 8

    def group(jj, hs):
        j = ng - 1 - jj if reverse else jj
        off = pl.multiple_of(j * 8, 8)
        out = []
        for bi in range(nb):
            a = a_s[bi, pl.ds(off, 8), :]
            b = b_s[bi, pl.ds(off, 8), :]
            for k in (1, 2):
                shift = 8 - k if reverse else k
                keep = (row < 8 - k) if reverse else (row >= k)
                a_sh = pltpu.roll(a, shift, 0)
                b_sh = pltpu.roll(b, shift, 0)
                b = jnp.where(keep, a * b_sh + b, b)
                a = jnp.where(keep, a * a_sh, a)
            near = b + a * hs[bi]
            far = a * pltpu.roll(near, 4, 0) + b
            hj = jnp.where((row >= 4) if reverse else (row < 4), near, far)
            b_s[bi, pl.ds(off, 8), :] = hj
            out.append(hj[0:1, :] if reverse else hj[7:8, :])
        return tuple(out)

    h_last = lax.fori_loop(0, ng, group, tuple(h_s[bi] for bi in range(nb)), unroll=4)
    for bi in range(nb):
        h_s[bi] = h_last[bi]
        hl_ref[bi] = jnp.broadcast_to(h_last[bi], hl_ref.shape[1:])
        if combine:
            out_ref[bi] = ((ro_ref[bi] + b_s[bi]) * _gelu_tanh(g_ref[bi])).astype(out_ref.dtype)
        else:
            out_ref[bi] = b_s[bi]


def _lru_direction(u, h0, cw, cb, wa_bd, wi_bd, ba, bi, lam, reverse, other=None, gate=None):
    bsz, t, c = u.shape
    tt = min(LRU_TILE, t)
    nt = t // tt
    g8 = tt // 8
    nb = LRU_BATCH if bsz % LRU_BATCH == 0 else 1
    combine = other is not None
    tile = (lambda s: nt - 1 - s) if reverse else (lambda s: s)
    cur = pl.BlockSpec((nb, tt, c), lambda b, s: (b, tile(s), 0))
    state = pl.BlockSpec((nb, 8, c), lambda b, s: (b, 0, 0))
    in_specs = [pl.BlockSpec((nb, 8, c), lambda b, s: (b, jnp.maximum(tile(s) * g8 - 1, 0), 0)),
                cur,
                pl.BlockSpec((nb, 8, c), lambda b, s: (b, jnp.minimum((tile(s) + 1) * g8, t // 8 - 1), 0)),
                state,
                _const_spec(cw.shape), _const_spec((1, c)),
                _const_spec(wa_bd.shape), _const_spec(wi_bd.shape),
                _const_spec((1, c)), _const_spec((1, c)), _const_spec((1, c))]
    args = [u, u, u, h0, cw, cb.reshape(1, c), wa_bd, wi_bd, ba.reshape(1, c), bi.reshape(1, c),
            lam.reshape(1, c)]
    if combine:
        in_specs += [cur, cur]
        args += [other, gate]
    return pl.pallas_call(
        functools.partial(_lru_kernel, reverse=reverse, combine=combine, tt=tt, nt=nt),
        grid=(bsz // nb, nt),
        in_specs=in_specs,
        out_specs=[cur, state],
        out_shape=[jax.ShapeDtypeStruct((bsz, t, c), BF16 if combine else F32),
                   jax.ShapeDtypeStruct((bsz, 8, c), F32)],
        scratch_shapes=[pltpu.VMEM((nb, tt + 16, c), F32), pltpu.VMEM((nb, tt, c), F32),
                        pltpu.VMEM((nb, tt, c), F32), pltpu.VMEM((nb, 1, c), F32)],
        compiler_params=_cparams("parallel", "arbitrary"),
        name="lru_bwd" if reverse else "lru_fwd",
    )(*args)


def _block_diag_halves(w):
    h, d, _ = w.shape
    per = h // 2
    eye = jnp.eye(per, dtype=w.dtype)
    wh = w.reshape(2, per, d, d)
    bd = jnp.einsum("pq,cpij->cpiqj", eye, wh).reshape(2, per * d, per * d)
    return bd.astype(BF16)


def _fourier_kernel(x_ref, cs_ref, tc_ref, ts_ref, cc_ref, sc_ref, out_ref, y_s, *, n2):
    radix, _, cw = y_s.shape
    for n1 in range(radix):
        x = jnp.concatenate([x_ref[j, 0, pl.ds(n1, n2, stride=radix), :] for j in range(x_ref.shape[0])],
                            axis=-1)
        y_s[n1] = jnp.dot(cs_ref[...], x.astype(BF16), preferred_element_type=F32)

    def group(g, carry):
        r = pl.multiple_of(g * 8, 8)
        halves = [slice(h * LANES, (h + 1) * LANES) for h in range(cw // LANES)]
        loaded = [[(y_s[n1, pl.ds(r, 8), ls], y_s[n1, pl.ds(n2 + r, 8), ls]) for n1 in range(radix)]
                  for ls in halves]
        for ls, ys_half in zip(halves, loaded):
            zr, zi = [], []
            for n1, (yc, ys) in enumerate(ys_half):
                if n1 == 0:
                    zr.append(yc)
                    zi.append(-ys)
                else:
                    tc = tc_ref[n1, pl.ds(r, 8), :]
                    ts = ts_ref[n1, pl.ds(r, 8), :]
                    zr.append(yc * tc - ys * ts)
                    zi.append(-(yc * ts + ys * tc))
            ar, ai = _dft8(zr, zi)
            for k1 in range(radix):
                y_s[k1, pl.ds(r, 8), ls] = ar[k1]
                y_s[k1, pl.ds(n2 + r, 8), ls] = ai[k1]
        return carry

    lax.fori_loop(0, n2 // 8, group, 0)

    for k1 in range(radix):
        o = (jnp.dot(y_s[k1, 0:n2, :].astype(BF16), cc_ref[...], preferred_element_type=F32)
             + jnp.dot(y_s[k1, n2:2 * n2, :].astype(BF16), sc_ref[...], preferred_element_type=F32))
        out_ref[0, k1 * n2:(k1 + 1) * n2, :] = o.astype(out_ref.dtype)


def _dft8(zr, zi):
    h = math.sqrt(0.5)

    def dft4(r, i):
        s0r, s0i = r[0] + r[2], i[0] + i[2]
        s1r, s1i = r[0] - r[2], i[0] - i[2]
        s2r, s2i = r[1] + r[3], i[1] + i[3]
        s3r, s3i = r[1] - r[3], i[1] - i[3]
        return ([s0r + s2r, s1r + s3i, s0r - s2r, s1r - s3i],
                [s0i + s2i, s1i - s3r, s0i - s2i, s1i + s3r])

    er, ei = dft4(zr[0::2], zi[0::2])
    odr, odi = dft4(zr[1::2], zi[1::2])
    tr = [odr[0], h * (odr[1] + odi[1]), odi[2], h * (odi[3] - odr[3])]
    ti = [odi[0], h * (odi[1] - odr[1]), -odr[2], -h * (odr[3] + odi[3])]
    return ([er[k] + tr[k] for k in range(4)] + [er[k] - tr[k] for k in range(4)],
            [ei[k] + ti[k] for k in range(4)] + [ei[k] - ti[k] for k in range(4)])


def _fourier(f, group_dim):
    ns, bsz, t, _ = f.shape
    c = ns * LANES
    radix = DFT_RADIX
    n2 = t // radix
    cw = MXU_DIM
    nh = c // cw
    spb = cw // LANES
    k = np.arange(n2)
    ang = 2.0 * np.pi * ((k[:, None] * k[None, :]) % n2) / n2
    cs = np.concatenate([np.cos(ang), np.sin(ang)], 0) / math.sqrt(t)
    n1 = np.arange(radix)
    ang_t = 2.0 * np.pi * ((n1[:, None] * k[None, :]) % t) / t
    tw_shape = (radix, n2, LANES)
    j = np.arange(cw)
    same = (j[:, None] // group_dim) == (j[None, :] // group_dim)
    ang_c = 2.0 * np.pi * ((j[:, None] * j[None, :]) % group_dim) / group_dim
    ccm = np.where(same, np.cos(ang_c), 0.0) / math.sqrt(group_dim)
    scm = np.where(same, np.sin(ang_c), 0.0) / math.sqrt(group_dim)
    assert radix == 8, "_dft8 is the radix stage"
    return pl.pallas_call(
        functools.partial(_fourier_kernel, n2=n2),
        grid=(bsz, nh),
        in_specs=[pl.BlockSpec((spb, 1, t, LANES), lambda b, h: (h, b, 0, 0)),
                  _const_spec((2 * n2, n2)),
                  _const_spec(tw_shape), _const_spec(tw_shape),
                  _const_spec((cw, cw)), _const_spec((cw, cw))],
        out_specs=pl.BlockSpec((1, t, cw), lambda b, h: (b, 0, h), pipeline_mode=pl.Buffered(1)),
        out_shape=jax.ShapeDtypeStruct((bsz, t, c), BF16),
        scratch_shapes=[pltpu.VMEM((radix, 2 * n2, cw), F32)],
        compiler_params=_cparams("parallel", "parallel"),
        name="fourier",
    )(f,
      jnp.asarray(cs, F32).astype(BF16),
      jnp.broadcast_to(jnp.asarray(np.cos(ang_t), F32)[:, :, None], tw_shape),
      jnp.broadcast_to(jnp.asarray(np.sin(ang_t), F32)[:, :, None], tw_shape),
      jnp.asarray(ccm, F32).astype(BF16), jnp.asarray(scm, F32).astype(BF16))


def _natten_kernel(q_ref, kt_ref, v_ref, kct_ref, vc_ref, tz_ref, o_ref, bias_s, *, qr_n, kh):
    rows, w, lanes = q_ref.shape[1:]
    nrb = rows // qr_n
    hd = lanes // 2
    half_n = qr_n // 2
    band = half_n + kh
    nq = half_n * w
    lane = lax.broadcasted_iota(jnp.int32, (1, lanes), 1)
    first = lane < hd
    ones = jnp.ones((1, lanes), BF16)
    mask = [(lane // hd == hh).astype(F32).astype(BF16) for hh in range(2)]
    vc = vc_ref[0]
    vc1 = jnp.concatenate([vc, jnp.broadcast_to(ones, vc.shape)], axis=1)

    def band_start(r_first):
        return jnp.clip(r_first - kh // 2, 0, rows - band)

    geometry_blocks = (0, 1, nrb - 1)

    def geometry(rb):
        return jnp.where(rb == 0, 0, jnp.where(rb == nrb - 1, 2, 1))

    def build_bias(geom):
        rb = geometry_blocks[geom]
        for half in range(2):
            r_first = rb * qr_n + half * half_n
            kr0 = min(max(r_first - kh // 2, 0), rows - band)
            for qr in range(half_n):
                r = r_first + qr
                rs = min(max(r - kh // 2, 0), rows - kh)
                for m in range(band // 2):
                    kr = kr0 + 2 * m
                    ae = kr - r + (NA_KH - 1) + NA_KH
                    kr_lane = kr + lane // hd
                    ok = (kr_lane >= rs) & (kr_lane < rs + kh)
                    for hh in range(2):
                        row0 = hh * nq + qr * w
                        bias_s[geom, half, row0:row0 + w, m * lanes:(m + 1) * lanes] = jnp.where(
                            ok, tz_ref[hh, ae], NEG)

    def scores(rb, half):
        r_first = rb * qr_n + half * half_n
        kr0 = band_start(r_first)
        q = q_ref[0, pl.ds(r_first, half_n)].reshape(nq, lanes)
        qq = jnp.concatenate([q * mask[0], q * mask[1]], axis=0)
        kt = kt_ref[:, pl.ds(pl.multiple_of(kr0 * w, MXU_DIM), band * w)]
        s_loc = jnp.dot(qq, kt, preferred_element_type=F32) + bias_s[geometry(rb), half]
        s_ctx = jnp.dot(qq, kct_ref[...], preferred_element_type=F32)
        return s_loc, s_ctx

    def softmax(s):
        s_loc, s_ctx = s
        mx = jnp.maximum(jnp.max(s_loc, axis=-1, keepdims=True), jnp.max(s_ctx, axis=-1, keepdims=True))
        return jnp.exp2((s_loc - mx).astype(BF16)), jnp.exp2((s_ctx - mx).astype(BF16))

    def attend(rb, half, p):
        r_first = rb * qr_n + half * half_n
        vb = v_ref[0, pl.ds(band_start(r_first), band)].reshape(band * w, lanes)
        vb1 = jnp.concatenate([vb, jnp.broadcast_to(ones, vb.shape)], axis=1)
        acc = (jnp.dot(p[0], vb1, preferred_element_type=F32)
               + jnp.dot(p[1], vc1, preferred_element_type=F32))
        o0 = acc[:nq, :lanes] / acc[:nq, lanes:]
        o1 = acc[nq:, :lanes] / acc[nq:, lanes:]
        o_ref[0, pl.ds(r_first, half_n)] = jnp.where(first, o0, o1).reshape(
            half_n, w, lanes).astype(o_ref.dtype)

    @pl.when(pl.program_id(1) == 0)
    def _():
        for geom in range(len(geometry_blocks)):
            build_bias(geom)

    def row_block(rb, carry):
        s_a = scores(rb, 0)
        s_b = scores(rb, 1)
        p_a = softmax(s_a)
        attend(rb, 0, p_a)
        p_b = softmax(s_b)
        attend(rb, 1, p_b)
        return carry

    lax.fori_loop(0, nrb, row_block, 0, unroll=2)


def _natten_bias_table(rpb):
    h = rpb.shape[0]
    w = GRID_W
    qc = np.arange(w)
    cs = np.clip(qc - NA_KW // 2, 0, w - NA_KW)
    kc = np.arange(w)
    inside = (kc[None, :] >= cs[:, None]) & (kc[None, :] < cs[:, None] + NA_KW)
    dc = kc[None, :] - qc[:, None] + NA_KW - 1
    pick = (np.arange(2 * NA_KW - 1)[:, None, None] == dc[None]) & inside[None]
    slabs = jnp.einsum("had,dqk->haqk", LOG2E * rpb, jnp.asarray(pick, F32),
                       precision=lax.Precision.HIGHEST)
    slabs = slabs + jnp.asarray(np.where(inside, 0.0, NEG), F32)
    n_slab = 4 * NA_KH + 1
    pad_lo = NA_KH
    pad_hi = n_slab - pad_lo - slabs.shape[1]
    slabs = jnp.pad(slabs, ((0, 0), (pad_lo, pad_hi), (0, 0), (0, 0)), constant_values=NEG)
    return jnp.concatenate([slabs[:, :-1], slabs[:, 1:]], axis=-1)


def _natten(q, kt, v, kct, vc, table):
    bsz, rows, w, d = q.shape
    l = vc.shape[1]
    qr_n = NA_KH
    kh = min(NA_KH, rows)
    band = qr_n // 2 + kh
    assert w == GRID_W and 2 * w == LANES and rows >= 2 * qr_n and rows % qr_n == 0
    assert (qr_n // 2 * w) % MXU_DIM == 0 and l % LANES == 0
    nhp = d // LANES
    image = pl.BlockSpec((1, rows, w, LANES), lambda hp, b: (b, 0, 0, hp))
    return pl.pallas_call(
        functools.partial(_natten_kernel, qr_n=qr_n, kh=kh),
        grid=(nhp, bsz),
        in_specs=[image,
                  pl.BlockSpec((LANES, rows * w), lambda hp, b: (hp, b)),
                  image,
                  pl.BlockSpec((LANES, l), lambda hp, b: (hp, b)),
                  pl.BlockSpec((1, l, LANES), lambda hp, b: (b, 0, hp)),
                  pl.BlockSpec((2,) + table.shape[1:], lambda hp, b: (hp, 0, 0, 0))],
        out_specs=image,
        out_shape=jax.ShapeDtypeStruct(q.shape, BF16),
        scratch_shapes=[pltpu.VMEM((3, 2, qr_n * w, band * w), F32)],
        compiler_params=_cparams("parallel", "arbitrary"),
        name="natten",
    )(q, kt, v, kct, vc, table)


def _tail_kernel(*refs, n_lhs, f_chunk):
    lhs = refs[:n_lhs]
    (wo_ref, x_ref, gpm_ref, gt1_ref, gpf_ref, sh2_ref, sc2_ref, wg_ref, wu_ref, wd_ref, gpo_ref,
     gt2_ref, o_ref) = refs[n_lhs:]
    hm = x_ref.shape[0] // 2

    def outproj(rs):
        y, k0 = None, 0
        for a_ref in lhs:
            k1 = k0 + a_ref.shape[1]
            part = jnp.dot(a_ref[rs, :], wo_ref[k0:k1, :], preferred_element_type=F32)
            y = part if y is None else y + part
            k0 = k1
        return y

    def norms(rs, y):
        x1 = x_ref[rs, :] + gt1_ref[0] * _rms(y, gpm_ref[...])
        fx = (_rms(x1, gpf_ref[...]) * (1.0 + sc2_ref[0]) + sh2_ref[0]).astype(BF16)
        return x1, fx

    def ffn(fx):
        acc = None
        for lo in range(0, wg_ref.shape[1], f_chunk):
            gate = jnp.dot(fx, wg_ref[:, lo:lo + f_chunk], preferred_element_type=F32)
            up = jnp.dot(fx, wu_ref[:, lo:lo + f_chunk], preferred_element_type=F32)
            hid = (_silu(gate) * up).astype(BF16)
            part = jnp.dot(hid, wd_ref[lo:lo + f_chunk, :], preferred_element_type=F32)
            acc = part if acc is None else acc + part
        return acc

    ra, rb = slice(0, hm), slice(hm, 2 * hm)
    ya = outproj(ra)
    yb = outproj(rb)
    x1a, fxa = norms(ra, ya)
    fa = ffn(fxa)
    x1b, fxb = norms(rb, yb)
    o_ref[ra, :] = x1a + gt2_ref[0] * _rms(fa, gpo_ref[...])
    fb = ffn(fxb)
    o_ref[rb, :] = x1b + gt2_ref[0] * _rms(fb, gpo_ref[...])


def _tail(lhs, wo, x2, gpm, gt1, gpf, sh2, sc2, wg, wu, wd, gpo, gt2, rows_per_cond, cond_base):
    n, d = x2.shape
    f = wg.shape[1]
    tm = min(TAIL_TILE, n)
    assert f % TAIL_F_CHUNK == 0
    tiles_per_cond = rows_per_cond // tm
    cond_map = lambda i: (cond_base + i // tiles_per_cond, 0, 0)
    cond = pl.BlockSpec((1, 1, d), cond_map)
    vec = _const_spec((1, d))
    return pl.pallas_call(
        functools.partial(_tail_kernel, n_lhs=len(lhs), f_chunk=TAIL_F_CHUNK),
        grid=(n // tm,),
        in_specs=([pl.BlockSpec((tm, a.shape[1]), lambda i: (i, 0)) for a in lhs]
                  + [_const_spec(wo.shape),
                     pl.BlockSpec((tm, d), lambda i: (i, 0)), vec, cond, vec, cond, cond,
                     _const_spec(wg.shape), _const_spec(wu.shape), _const_spec(wd.shape), vec, cond]),
        out_specs=pl.BlockSpec((tm, d), lambda i: (i, 0)),
        out_shape=jax.ShapeDtypeStruct((n, d), F32),
        compiler_params=_cparams("parallel"),
        name="tail",
    )(*lhs, wo, x2, gpm.reshape(1, d), gt1, gpf.reshape(1, d), sh2, sc2, wg, wu, wd,
      gpo.reshape(1, d), gt2)


def kernel(x, c, ctx, c_ctx, w_mod, b_mod, g_pre_mix, g_post_mix, g_pre_ffn, g_post_ffn, w_ffn_gate, w_ffn_up, w_ffn_down, w_in_ab, conv_w, conv_b, lru_w_a, lru_b_a, lru_w_i, lru_b_i, lru_lam, w_out_ab, w_qkv_na, rpb_na, w_out_na):
    bsz, seq, d = x.shape
    ctx_len = ctx.shape[1]
    depth = w_mod.shape[0]
    lru_w = conv_w.shape[-1]
    n_cond = 8
    assert bsz < n_cond
    ctx_cond = bsz

    cond = jnp.concatenate([c, c_ctx[None], jnp.zeros((n_cond - bsz - 1, d), F32)], 0)
    mod = _mod_vectors(cond, w_mod, b_mod).reshape(depth, n_cond, 6, 1, d)

    xs = x.reshape(bsz * seq, d)
    cs = ctx.reshape(bsz * ctx_len, d)
    for l in range(depth):
        last = l == depth - 1
        sh1, sc1, gt1, sh2, sc2, gt2 = (mod[l, :, j] for j in range(6))
        wg, wu, wd = (w[l].astype(BF16) for w in (w_ffn_gate, w_ffn_up, w_ffn_down))
        norm_args = (g_post_mix[l], gt1, g_pre_ffn[l], sh2, sc2, wg, wu, wd, g_post_ffn[l], gt2)
        if l % 2 == 0:
            e = l // 2
            w_in = w_in_ab[e].astype(BF16)
            w_out = w_out_ab[e].astype(BF16)
            splits = ((0, lru_w, 1.0), (lru_w, 2 * lru_w, 1.0), (2 * lru_w, w_in.shape[1], 1.0))
            gates = [(_block_diag_halves(0.5 * lru_w_a[e, dr]), _block_diag_halves(0.5 * lru_w_i[e, dr]))
                     for dr in range(2)]

            def mixer(tokens, t_len, rows_per_cond, cond_base, h0):
                u, g, f = _inproj(tokens, g_pre_mix[l], sh1, sc1, w_in, splits, (F32, F32, F32),
                                  rows_per_cond, cond_base, slabbed=(2,))
                u, g = (a.reshape(bsz, t_len, -1) for a in (u, g))
                f = f.reshape(f.shape[0], bsz, t_len, LANES)
                lru = functools.partial(_lru_direction, u, cw=conv_w[e], cb=conv_b[e])
                hf, end_f = lru(h0[0], wa_bd=gates[0][0], wi_bd=gates[0][1], ba=0.5 * lru_b_a[e, 0],
                                bi=0.5 * lru_b_i[e, 0], lam=lru_lam[e, 0], reverse=False)
                lr, end_b = lru(h0[1], wa_bd=gates[1][0], wi_bd=gates[1][1], ba=0.5 * lru_b_a[e, 1],
                                bi=0.5 * lru_b_i[e, 1], lam=lru_lam[e, 1], reverse=True, other=hf, gate=g)
                fo = _fourier(f, f.shape[0] * LANES // FFT_GROUPS)
                lhs = [lr.reshape(tokens.shape[0], -1), fo.reshape(tokens.shape[0], -1)]
                return lhs, (end_f, end_b)

            zeros = jnp.zeros((bsz, 8, lru_w), F32)
            lhs_c, ends = mixer(cs, ctx_len, bsz * ctx_len, ctx_cond, (zeros, zeros))
            lhs_x, _ = mixer(xs, seq, seq, 0, ends)
            if not last:
                cs = _tail(lhs_c, w_out, cs, *norm_args, bsz * ctx_len, ctx_cond)
            xs = _tail(lhs_x, w_out, xs, *norm_args, seq, 0)
        else:
            o = l // 2
            w_qkv = w_qkv_na[o].astype(BF16)
            n_heads = rpb_na.shape[1]
            scale = LOG2E * (d // n_heads) ** -0.5
            q_split, v_split = (0, d, scale), (2 * d, 3 * d, 1.0)
            w_k_t = w_qkv_na[o][:, d:2 * d].T.astype(BF16)
            q, v, kt = _inproj(xs, g_pre_mix[l], sh1, sc1, w_qkv, (q_split, v_split), (BF16, BF16),
                               seq, 0, wt=w_k_t)
            vc, kct = _inproj(cs, g_pre_mix[l], sh1, sc1, w_qkv, (v_split,), (BF16,),
                              bsz * ctx_len, ctx_cond, wt=w_k_t)
            grid = (bsz, seq // GRID_W, GRID_W, d)
            att = _natten(q.reshape(grid), kt, v.reshape(grid), kct, vc.reshape(bsz, ctx_len, d),
                          _natten_bias_table(rpb_na[o]))
            if not last:
                raise NotImplementedError("context update after an attention layer")
            xs = _tail([att.reshape(bsz * seq, d)], w_out_na[o].astype(BF16), xs, *norm_args, seq, 0)
    return xs.reshape(bsz, seq, d)
```

```python
import functools
import math

import numpy as np
import jax
import jax.numpy as jnp
from jax import lax
from jax.experimental import pallas as pl
from jax.experimental.pallas import tpu as pltpu

F32 = jnp.float32
BF16 = jnp.bfloat16

EPS = 1e-6
LRU_C = 8.0
CONV_LEFT = 2
FFT_GROUPS = 8
GRID_W = 64
NA_KH = 8
NA_KW = 16
DFT_RADIX = 8
NEG = -1e30
LOG2E = math.log2(math.e)

VMEM_LIMIT_BYTES = 56 * 1024 * 1024
ROW_TILE = 1024
LRU_TILE = 512
LRU_BATCH = 2
TAIL_TILE = 1024
TAIL_F_CHUNK = 256
LANES = 128
MXU_DIM = 256


def _cparams(*sem):
    return pltpu.CompilerParams(dimension_semantics=sem, vmem_limit_bytes=VMEM_LIMIT_BYTES)


def _const_spec(shape):
    nd = len(shape)
    return pl.BlockSpec(shape, lambda *_: (0,) * nd, pipeline_mode=pl.Buffered(1))


def _sigmoid(x):
    return 0.5 * (1.0 + jnp.tanh(0.5 * x))


def _silu(x):
    return x * _sigmoid(x)


def _gelu_tanh(x):
    c = math.sqrt(2.0 / math.pi)
    return 0.5 * x * (1.0 + jnp.tanh(c * (x + 0.044715 * (x * x * x))))


def _rms(x, g):
    return x * lax.rsqrt(jnp.mean(x * x, axis=-1, keepdims=True) + EPS) * g


def _mod_kernel(cond_ref, w_ref, b_ref, o_ref):
    act = _silu(cond_ref[...]).astype(BF16)
    o_ref[0] = jnp.dot(act, w_ref[0].astype(BF16), preferred_element_type=F32) + b_ref[0]


def _mod_vectors(cond, w_mod, b_mod):
    depth, d, n = w_mod.shape
    tn = n // 4
    return pl.pallas_call(
        _mod_kernel,
        grid=(depth, n // tn),
        in_specs=[pl.BlockSpec(cond.shape, lambda l, j: (0, 0)),
                  pl.BlockSpec((1, d, tn), lambda l, j: (l, 0, j)),
                  pl.BlockSpec((1, 1, tn), lambda l, j: (l, 0, j))],
        out_specs=pl.BlockSpec((1, cond.shape[0], tn), lambda l, j: (l, 0, j)),
        out_shape=jax.ShapeDtypeStruct((depth, cond.shape[0], n), F32),
        compiler_params=_cparams("parallel", "parallel"),
        name="mod",
    )(cond, w_mod, b_mod.reshape(depth, 1, n))


def _inproj_kernel(x_ref, g_ref, sh_ref, sc_ref, w_ref, *rest, splits, transposed):
    o_refs = rest
    if transposed:
        wt_ref, o_refs, ot_ref = rest[0], rest[1:-1], rest[-1]
    hm = x_ref.shape[0] // 2

    def norm(rs):
        h = _rms(x_ref[rs, :], g_ref[...]) * (1.0 + sc_ref[0]) + sh_ref[0]
        return h.astype(BF16)

    def project(rs, hb):
        if transposed:
            ot_ref[:, rs] = lax.dot_general(wt_ref[...], hb, (((1,), (1,)), ((), ())),
                                            preferred_element_type=F32).astype(ot_ref.dtype)
        for o_ref, (lo, hi, scale) in zip(o_refs, splits):
            z = jnp.dot(hb, w_ref[:, lo:hi], preferred_element_type=F32)
            if scale != 1.0:
                z = z * scale
            if len(o_ref.shape) == 3:
                for j in range(o_ref.shape[0]):
                    o_ref[j, rs, :] = z[:, j * LANES:(j + 1) * LANES].astype(o_ref.dtype)
            else:
                o_ref[rs, :] = z.astype(o_ref.dtype)

    ra, rb = slice(0, hm), slice(hm, 2 * hm)
    ha = norm(ra)
    hb = norm(rb)
    project(ra, ha)
    project(rb, hb)


def _inproj(x2, g, sh, sc, w, splits, dtypes, rows_per_cond, cond_base, slabbed=(), wt=None):
    n, d = x2.shape
    tm = min(ROW_TILE, n)
    tiles_per_cond = rows_per_cond // tm
    cond_map = lambda i: (cond_base + i // tiles_per_cond, 0, 0)
    out_specs, out_shape = [], []
    for j, ((lo, hi, _), dt) in enumerate(zip(splits, dtypes)):
        if j in slabbed:
            ns = (hi - lo) // LANES
            out_specs.append(pl.BlockSpec((ns, tm, LANES), lambda i: (0, i, 0)))
            out_shape.append(jax.ShapeDtypeStruct((ns, n, LANES), dt))
        else:
            out_specs.append(pl.BlockSpec((tm, hi - lo), lambda i: (i, 0)))
            out_shape.append(jax.ShapeDtypeStruct((n, hi - lo), dt))
    in_specs = [pl.BlockSpec((tm, d), lambda i: (i, 0)),
                _const_spec((1, d)),
                pl.BlockSpec((1, 1, d), cond_map),
                pl.BlockSpec((1, 1, d), cond_map),
                _const_spec(w.shape)]
    args = [x2, g.reshape(1, d), sh, sc, w]
    if wt is not None:
        in_specs.append(_const_spec(wt.shape))
        args.append(wt)
        out_specs.append(pl.BlockSpec((wt.shape[0], tm), lambda i: (0, i)))
        out_shape.append(jax.ShapeDtypeStruct((wt.shape[0], n), BF16))
    return pl.pallas_call(
        functools.partial(_inproj_kernel, splits=splits, transposed=wt is not None),
        grid=(n // tm,),
        in_specs=in_specs,
        out_specs=out_specs,
        out_shape=out_shape,
        compiler_params=_cparams("parallel"),
        name="inproj",
    )(*args)


def _lru_kernel(*refs, reverse, combine, tt, nt):
    if combine:
        (up_ref, u_ref, un_ref, h0_ref, cw_ref, cb_ref, wa_ref, wi_ref, ba_ref, bi_ref, lam_ref,
         ro_ref, g_ref, out_ref, hl_ref, ext_s, a_s, b_s, h_s) = refs
    else:
        (up_ref, u_ref, un_ref, h0_ref, cw_ref, cb_ref, wa_ref, wi_ref, ba_ref, bi_ref, lam_ref,
         out_ref, hl_ref, ext_s, a_s, b_s, h_s) = refs
    step = pl.program_id(1)
    tile = nt - 1 - step if reverse else step
    nb, _, c = u_ref.shape
    half = c // 2

    @pl.when(step == 0)
    def _():
        for bi in range(nb):
            h_s[bi] = h0_ref[bi, 0:1, :]

    lam = lam_ref[...]
    softplus_neg_lam = jnp.maximum(-lam, 0.0) + jnp.log1p(jnp.exp(-jnp.abs(lam)))
    coef = (-0.5 * LRU_C) * softplus_neg_lam
    for bi in range(nb):
        ext_s[bi, 0:8, :] = jnp.where(tile > 0, up_ref[bi], 0.0)
        ext_s[bi, 8:8 + tt, :] = u_ref[bi]
        ext_s[bi, 8 + tt:16 + tt, :] = jnp.where(tile < nt - 1, un_ref[bi], 0.0)
        ext = ext_s[bi]
        uc = cb_ref[...]
        for k in range(cw_ref.shape[0]):
            back = (CONV_LEFT - k) % (tt + 16)
            tap = ext if back == 0 else pltpu.roll(ext, back, 0)
            uc = uc + tap[8:8 + tt, :] * cw_ref[k:k + 1, :]

        ub = uc.astype(BF16)
        for hb in range(2):
            sl = slice(hb * half, (hb + 1) * half)
            tr = jnp.tanh(jnp.dot(ub[:, sl], wa_ref[hb], preferred_element_type=F32) + ba_ref[:, sl])
            ti = jnp.tanh(jnp.dot(ub[:, sl], wi_ref[hb], preferred_element_type=F32) + bi_ref[:, sl])
            log_a = coef[:, sl] * tr + coef[:, sl]
            a = jnp.exp(log_a)
            a_s[bi, :, sl] = a
            uh = 0.5 * uc[:, sl]
            w = -jnp.tanh(log_a) * (a * a + 1.0)
            root = jnp.where(w > 0.0, w * lax.rsqrt(w), 0.0)
            b_s[bi, :, sl] = root * (uh * ti + uh)

    row = lax.broadcasted_iota(jnp.int32, (8, c), 0)
    ng = tt // 8

    def group(jj, hs):
        j = ng - 1 - jj if reverse else jj
        off = pl.multiple_of(j * 8, 8)
        out = []
        for bi in range(nb):
            a = a_s[bi, pl.ds(off, 8), :]
            b = b_s[bi, pl.ds(off, 8), :]
            for k in (1, 2):
                shift = 8 - k if reverse else k
                keep = (row < 8 - k) if reverse else (row >= k)
                a_sh = pltpu.roll(a, shift, 0)
                b_sh = pltpu.roll(b, shift, 0)
                b = jnp.where(keep, a * b_sh + b, b)
                a = jnp.where(keep, a * a_sh, a)
            near = b + a * hs[bi]
            far = a * pltpu.roll(near, 4, 0) + b
            hj = jnp.where((row >= 4) if reverse else (row < 4), near, far)
            b_s[bi, pl.ds(off, 8), :] = hj
            out.append(hj[0:1, :] if reverse else hj[7:8, :])
        return tuple(out)

    h_last = lax.fori_loop(0, ng, group, tuple(h_s[bi] for bi in range(nb)), unroll=4)
    for bi in range(nb):
        h_s[bi] = h_last[bi]
        hl_ref[bi] = jnp.broadcast_to(h_last[bi], hl_ref.shape[1:])
        if combine:
            out_ref[bi] = ((ro_ref[bi] + b_s[bi]) * _gelu_tanh(g_ref[bi])).astype(out_ref.dtype)
        else:
            out_ref[bi] = b_s[bi]


def _lru_direction(u, h0, cw, cb, wa_bd, wi_bd, ba, bi, lam, reverse, other=None, gate=None):
    bsz, t, c = u.shape
    tt = min(LRU_TILE, t)
    nt = t // tt
    g8 = tt // 8
    nb = LRU_BATCH if bsz % LRU_BATCH == 0 else 1
    combine = other is not None
    tile = (lambda s: nt - 1 - s) if reverse else (lambda s: s)
    cur = pl.BlockSpec((nb, tt, c), lambda b, s: (b, tile(s), 0))
    state = pl.BlockSpec((nb, 8, c), lambda b, s: (b, 0, 0))
    in_specs = [pl.BlockSpec((nb, 8, c), lambda b, s: (b, jnp.maximum(tile(s) * g8 - 1, 0), 0)),
                cur,
                pl.BlockSpec((nb, 8, c), lambda b, s: (b, jnp.minimum((tile(s) + 1) * g8, t // 8 - 1), 0)),
                state,
                _const_spec(cw.shape), _const_spec((1, c)),
                _const_spec(wa_bd.shape), _const_spec(wi_bd.shape),
                _const_spec((1, c)), _const_spec((1, c)), _const_spec((1, c))]
    args = [u, u, u, h0, cw, cb.reshape(1, c), wa_bd, wi_bd, ba.reshape(1, c), bi.reshape(1, c),
            lam.reshape(1, c)]
    if combine:
        in_specs += [cur, cur]
        args += [other, gate]
    return pl.pallas_call(
        functools.partial(_lru_kernel, reverse=reverse, combine=combine, tt=tt, nt=nt),
        grid=(bsz // nb, nt),
        in_specs=in_specs,
        out_specs=[cur, state],
        out_shape=[jax.ShapeDtypeStruct((bsz, t, c), BF16 if combine else F32),
                   jax.ShapeDtypeStruct((bsz, 8, c), F32)],
        scratch_shapes=[pltpu.VMEM((nb, tt + 16, c), F32), pltpu.VMEM((nb, tt, c), F32),
                        pltpu.VMEM((nb, tt, c), F32), pltpu.VMEM((nb, 1, c), F32)],
        compiler_params=_cparams("parallel", "arbitrary"),
        name="lru_bwd" if reverse else "lru_fwd",
    )(*args)


def _block_diag_halves(w):
    h, d, _ = w.shape
    per = h // 2
    eye = jnp.eye(per, dtype=w.dtype)
    wh = w.reshape(2, per, d, d)
    bd = jnp.einsum("pq,cpij->cpiqj", eye, wh).reshape(2, per * d, per * d)
    return bd.astype(BF16)


def _fourier_kernel(x_ref, cs_ref, tc_ref, ts_ref, cc_ref, sc_ref, out_ref, y_s, *, n2):
    radix, _, cw = y_s.shape
    for n1 in range(radix):
        x = jnp.concatenate([x_ref[j, 0, pl.ds(n1, n2, stride=radix), :] for j in range(x_ref.shape[0])],
                            axis=-1)
        y_s[n1] = jnp.dot(cs_ref[...], x.astype(BF16), preferred_element_type=F32)

    def group(g, carry):
        r = pl.multiple_of(g * 8, 8)
        halves = [slice(h * LANES, (h + 1) * LANES) for h in range(cw // LANES)]
        loaded = [[(y_s[n1, pl.ds(r, 8), ls], y_s[n1, pl.ds(n2 + r, 8), ls]) for n1 in range(radix)]
                  for ls in halves]
        for ls, ys_half in zip(halves, loaded):
            zr, zi = [], []
            for n1, (yc, ys) in enumerate(ys_half):
                if n1 == 0:
                    zr.append(yc)
                    zi.append(-ys)
                else:
                    tc = tc_ref[n1, pl.ds(r, 8), :]
                    ts = ts_ref[n1, pl.ds(r, 8), :]
                    zr.append(yc * tc - ys * ts)
                    zi.append(-(yc * ts + ys * tc))
            ar, ai = _dft8(zr, zi)
            for k1 in range(radix):
                y_s[k1, pl.ds(r, 8), ls] = ar[k1]
                y_s[k1, pl.ds(n2 + r, 8), ls] = ai[k1]
        return carry

    lax.fori_loop(0, n2 // 8, group, 0)

    for k1 in range(radix):
        o = (jnp.dot(y_s[k1, 0:n2, :].astype(BF16), cc_ref[...], preferred_element_type=F32)
             + jnp.dot(y_s[k1, n2:2 * n2, :].astype(BF16), sc_ref[...], preferred_element_type=F32))
        out_ref[0, k1 * n2:(k1 + 1) * n2, :] = o.astype(out_ref.dtype)


def _dft8(zr, zi):
    h = math.sqrt(0.5)

    def dft4(r, i):
        s0r, s0i = r[0] + r[2], i[0] + i[2]
        s1r, s1i = r[0] - r[2], i[0] - i[2]
        s2r, s2i = r[1] + r[3], i[1] + i[3]
        s3r, s3i = r[1] - r[3], i[1] - i[3]
        return ([s0r + s2r, s1r + s3i, s0r - s2r, s1r - s3i],
                [s0i + s2i, s1i - s3r, s0i - s2i, s1i + s3r])

    er, ei = dft4(zr[0::2], zi[0::2])
    odr, odi = dft4(zr[1::2], zi[1::2])
    tr = [odr[0], h * (odr[1] + odi[1]), odi[2], h * (odi[3] - odr[3])]
    ti = [odi[0], h * (odi[1] - odr[1]), -odr[2], -h * (odr[3] + odi[3])]
    return ([er[k] + tr[k] for k in range(4)] + [er[k] - tr[k] for k in range(4)],
            [ei[k] + ti[k] for k in range(4)] + [ei[k] - ti[k] for k in range(4)])


def _fourier(f, group_dim):
    ns, bsz, t, _ = f.shape
    c = ns * LANES
    radix = DFT_RADIX
    n2 = t // radix
    cw = MXU_DIM
    nh = c // cw
    spb = cw // LANES
    k = np.arange(n2)
    ang = 2.0 * np.pi * ((k[:, None] * k[None, :]) % n2) / n2
    cs = np.concatenate([np.cos(ang), np.sin(ang)], 0) / math.sqrt(t)
    n1 = np.arange(radix)
    ang_t = 2.0 * np.pi * ((n1[:, None] * k[None, :]) % t) / t
    tw_shape = (radix, n2, LANES)
    j = np.arange(cw)
    same = (j[:, None] // group_dim) == (j[None, :] // group_dim)
    ang_c = 2.0 * np.pi * ((j[:, None] * j[None, :]) % group_dim) / group_dim
    ccm = np.where(same, np.cos(ang_c), 0.0) / math.sqrt(group_dim)
    scm = np.where(same, np.sin(ang_c), 0.0) / math.sqrt(group_dim)
    assert radix == 8, "_dft8 is the radix stage"
    return pl.pallas_call(
        functools.partial(_fourier_kernel, n2=n2),
        grid=(bsz, nh),
        in_specs=[pl.BlockSpec((spb, 1, t, LANES), lambda b, h: (h, b, 0, 0)),
                  _const_spec((2 * n2, n2)),
                  _const_spec(tw_shape), _const_spec(tw_shape),
                  _const_spec((cw, cw)), _const_spec((cw, cw))],
        out_specs=pl.BlockSpec((1, t, cw), lambda b, h: (b, 0, h), pipeline_mode=pl.Buffered(1)),
        out_shape=jax.ShapeDtypeStruct((bsz, t, c), BF16),
        scratch_shapes=[pltpu.VMEM((radix, 2 * n2, cw), F32)],
        compiler_params=_cparams("parallel", "parallel"),
        name="fourier",
    )(f,
      jnp.asarray(cs, F32).astype(BF16),
      jnp.broadcast_to(jnp.asarray(np.cos(ang_t), F32)[:, :, None], tw_shape),
      jnp.broadcast_to(jnp.asarray(np.sin(ang_t), F32)[:, :, None], tw_shape),
      jnp.asarray(ccm, F32).astype(BF16), jnp.asarray(scm, F32).astype(BF16))


def _natten_kernel(q_ref, kt_ref, v_ref, kct_ref, vc_ref, tz_ref, o_ref, bias_s, *, qr_n, kh):
    rows, w, lanes = q_ref.shape[1:]
    nrb = rows // qr_n
    hd = lanes // 2
    half_n = qr_n // 2
    band = half_n + kh
    nq = half_n * w
    lane = lax.broadcasted_iota(jnp.int32, (1, lanes), 1)
    first = lane < hd
    ones = jnp.ones((1, lanes), BF16)
    mask = [(lane // hd == hh).astype(F32).astype(BF16) for hh in range(2)]
    vc = vc_ref[0]
    vc1 = jnp.concatenate([vc, jnp.broadcast_to(ones, vc.shape)], axis=1)

    def band_start(r_first):
        return jnp.clip(r_first - kh // 2, 0, rows - band)

    geometry_blocks = (0, 1, nrb - 1)

    def geometry(rb):
        return jnp.where(rb == 0, 0, jnp.where(rb == nrb - 1, 2, 1))

    def build_bias(geom):
        rb = geometry_blocks[geom]
        for half in range(2):
            r_first = rb * qr_n + half * half_n
            kr0 = min(max(r_first - kh // 2, 0), rows - band)
            for qr in range(half_n):
                r = r_first + qr
                rs = min(max(r - kh // 2, 0), rows - kh)
                for m in range(band // 2):
                    kr = kr0 + 2 * m
                    in_window = [rs <= kr + j < rs + kh for j in range(2)]
                    pair = kr - r + NA_KH
                    for hh in range(2):
                        row0 = hh * nq + qr * w
                        if not any(in_window):
                            tile = jnp.full((w, lanes), NEG, F32)
                        elif all(in_window):
                            tile = tz_ref[hh, pair]
                        else:
                            keep = (lane < hd) if in_window[0] else (lane >= hd)
                            tile = jnp.where(keep, tz_ref[hh, pair], NEG)
                        bias_s[geom, half, row0:row0 + w, m * lanes:(m + 1) * lanes] = tile

    def scores(rb, half):
        r_first = rb * qr_n + half * half_n
        kr0 = band_start(r_first)
        q = q_ref[0, pl.ds(r_first, half_n)].reshape(nq, lanes)
        qq = jnp.concatenate([q * mask[0], q * mask[1]], axis=0)
        kt = kt_ref[:, pl.ds(pl.multiple_of(kr0 * w, MXU_DIM), band * w)]
        s_loc = jnp.dot(qq, kt, preferred_element_type=F32) + bias_s[geometry(rb), half]
        s_ctx = jnp.dot(qq, kct_ref[...], preferred_element_type=F32)
        return s_loc, s_ctx

    def softmax(s):
        s_loc, s_ctx = s
        mx = jnp.maximum(jnp.max(s_loc, axis=-1, keepdims=True), jnp.max(s_ctx, axis=-1, keepdims=True))
        return jnp.exp2((s_loc - mx).astype(BF16)), jnp.exp2((s_ctx - mx).astype(BF16))

    def attend(rb, half, p):
        r_first = rb * qr_n + half * half_n
        vb = v_ref[0, pl.ds(band_start(r_first), band)].reshape(band * w, lanes)
        vb1 = jnp.concatenate([vb, jnp.broadcast_to(ones, vb.shape)], axis=1)
        acc = (jnp.dot(p[0], vb1, preferred_element_type=F32)
               + jnp.dot(p[1], vc1, preferred_element_type=F32))
        o0 = acc[:nq, :lanes] / acc[:nq, lanes:]
        o1 = acc[nq:, :lanes] / acc[nq:, lanes:]
        o_ref[0, pl.ds(r_first, half_n)] = jnp.where(first, o0, o1).reshape(
            half_n, w, lanes).astype(o_ref.dtype)

    @pl.when(pl.program_id(1) == 0)
    def _():
        for geom in range(len(geometry_blocks)):
            build_bias(geom)

    def row_block(rb, carry):
        s_a = scores(rb, 0)
        s_b = scores(rb, 1)
        p_a = softmax(s_a)
        attend(rb, 0, p_a)
        p_b = softmax(s_b)
        attend(rb, 1, p_b)
        return carry

    lax.fori_loop(0, nrb, row_block, 0, unroll=2)


def _natten_bias_table(rpb):
    n_off = rpb.shape[1]
    n_dc = rpb.shape[2]
    w = GRID_W
    qc = np.arange(w)
    cs = np.clip(qc - NA_KW // 2, 0, w - NA_KW)
    kc = np.arange(w)
    inside = (kc[None, :] >= cs[:, None]) & (kc[None, :] < cs[:, None] + NA_KW)
    dc = kc[None, :] - qc[:, None] + NA_KW - 1
    pick = (np.arange(n_dc)[:, None, None] == dc[None]) & inside[None]
    pick2 = np.zeros((2 * n_dc, w, 2 * w), np.float32)
    pick2[:n_dc, :, :w] = pick
    pick2[n_dc:, :, w:] = pick
    ext = jnp.pad(LOG2E * rpb, ((0, 0), (1, 1), (0, 0)))
    rows2 = jnp.concatenate([ext[:, :-1], ext[:, 1:]], axis=-1)
    table = jnp.einsum("hed,dqj->heqj", rows2, jnp.asarray(pick2), precision=lax.Precision.HIGHEST)
    slab_ok = np.arange(n_off + 2) - 1
    slab_ok = (slab_ok >= 0) & (slab_ok < n_off)
    ok = np.concatenate([slab_ok[:-1, None, None] & inside[None], slab_ok[1:, None, None] & inside[None]],
                        axis=-1)
    return table + jnp.asarray(np.where(ok, 0.0, NEG), F32)


def _natten(q, kt, v, kct, vc, table):
    bsz, rows, w, d = q.shape
    l = vc.shape[1]
    qr_n = NA_KH
    kh = min(NA_KH, rows)
    band = qr_n // 2 + kh
    assert w == GRID_W and 2 * w == LANES and rows % (2 * qr_n) == 0
    assert (qr_n // 2 * w) % MXU_DIM == 0 and l % LANES == 0
    nhp = d // LANES
    image = pl.BlockSpec((1, rows, w, LANES), lambda hp, b: (b, 0, 0, hp))
    return pl.pallas_call(
        functools.partial(_natten_kernel, qr_n=qr_n, kh=kh),
        grid=(nhp, bsz),
        in_specs=[image,
                  pl.BlockSpec((LANES, rows * w), lambda hp, b: (hp, b)),
                  image,
                  pl.BlockSpec((LANES, l), lambda hp, b: (hp, b)),
                  pl.BlockSpec((1, l, LANES), lambda hp, b: (b, 0, hp)),
                  pl.BlockSpec((2,) + table.shape[1:], lambda hp, b: (hp, 0, 0, 0))],
        out_specs=image,
        out_shape=jax.ShapeDtypeStruct(q.shape, BF16),
        scratch_shapes=[pltpu.VMEM((3, 2, qr_n * w, band * w), F32)],
        compiler_params=_cparams("parallel", "arbitrary"),
        name="natten",
    )(q, kt, v, kct, vc, table)


def _tail_kernel(*refs, n_lhs, f_chunk):
    lhs = refs[:n_lhs]
    (wo_ref, x_ref, gpm_ref, gt1_ref, gpf_ref, sh2_ref, sc2_ref, wg_ref, wu_ref, wd_ref, gpo_ref,
     gt2_ref, o_ref) = refs[n_lhs:]
    hm = x_ref.shape[0] // 2

    def outproj(rs):
        y, k0 = None, 0
        for a_ref in lhs:
            k1 = k0 + a_ref.shape[1]
            part = jnp.dot(a_ref[rs, :], wo_ref[k0:k1, :], preferred_element_type=F32)
            y = part if y is None else y + part
            k0 = k1
        return y

    def norms(rs, y):
        x1 = x_ref[rs, :] + gt1_ref[0] * _rms(y, gpm_ref[...])
        fx = (_rms(x1, gpf_ref[...]) * (1.0 + sc2_ref[0]) + sh2_ref[0]).astype(BF16)
        return x1, fx

    def ffn(fx):
        acc = None
        for lo in range(0, wg_ref.shape[1], f_chunk):
            gate = jnp.dot(fx, wg_ref[:, lo:lo + f_chunk], preferred_element_type=F32)
            up = jnp.dot(fx, wu_ref[:, lo:lo + f_chunk], preferred_element_type=F32)
            hid = (_silu(gate) * up).astype(BF16)
            part = jnp.dot(hid, wd_ref[lo:lo + f_chunk, :], preferred_element_type=F32)
            acc = part if acc is None else acc + part
        return acc

    ra, rb = slice(0, hm), slice(hm, 2 * hm)
    ya = outproj(ra)
    yb = outproj(rb)
    x1a, fxa = norms(ra, ya)
    fa = ffn(fxa)
    x1b, fxb = norms(rb, yb)
    o_ref[ra, :] = x1a + gt2_ref[0] * _rms(fa, gpo_ref[...])
    fb = ffn(fxb)
    o_ref[rb, :] = x1b + gt2_ref[0] * _rms(fb, gpo_ref[...])


def _tail(lhs, wo, x2, gpm, gt1, gpf, sh2, sc2, wg, wu, wd, gpo, gt2, rows_per_cond, cond_base):
    n, d = x2.shape
    f = wg.shape[1]
    tm = min(TAIL_TILE, n)
    assert f % TAIL_F_CHUNK == 0
    tiles_per_cond = rows_per_cond // tm
    cond_map = lambda i: (cond_base + i // tiles_per_cond, 0, 0)
    cond = pl.BlockSpec((1, 1, d), cond_map)
    vec = _const_spec((1, d))
    return pl.pallas_call(
        functools.partial(_tail_kernel, n_lhs=len(lhs), f_chunk=TAIL_F_CHUNK),
        grid=(n // tm,),
        in_specs=([pl.BlockSpec((tm, a.shape[1]), lambda i: (i, 0)) for a in lhs]
                  + [_const_spec(wo.shape),
                     pl.BlockSpec((tm, d), lambda i: (i, 0)), vec, cond, vec, cond, cond,
                     _const_spec(wg.shape), _const_spec(wu.shape), _const_spec(wd.shape), vec, cond]),
        out_specs=pl.BlockSpec((tm, d), lambda i: (i, 0)),
        out_shape=jax.ShapeDtypeStruct((n, d), F32),
        compiler_params=_cparams("parallel"),
        name="tail",
    )(*lhs, wo, x2, gpm.reshape(1, d), gt1, gpf.reshape(1, d), sh2, sc2, wg, wu, wd,
      gpo.reshape(1, d), gt2)


def kernel(x, c, ctx, c_ctx, w_mod, b_mod, g_pre_mix, g_post_mix, g_pre_ffn, g_post_ffn, w_ffn_gate, w_ffn_up, w_ffn_down, w_in_ab, conv_w, conv_b, lru_w_a, lru_b_a, lru_w_i, lru_b_i, lru_lam, w_out_ab, w_qkv_na, rpb_na, w_out_na):
    bsz, seq, d = x.shape
    ctx_len = ctx.shape[1]
    depth = w_mod.shape[0]
    lru_w = conv_w.shape[-1]
    n_cond = 8
    assert bsz < n_cond
    ctx_cond = bsz

    cond = jnp.concatenate([c, c_ctx[None], jnp.zeros((n_cond - bsz - 1, d), F32)], 0)
    mod = _mod_vectors(cond, w_mod, b_mod).reshape(depth, n_cond, 6, 1, d)

    xs = x.reshape(bsz * seq, d)
    cs = ctx.reshape(bsz * ctx_len, d)
    for l in range(depth):
        last = l == depth - 1
        sh1, sc1, gt1, sh2, sc2, gt2 = (mod[l, :, j] for j in range(6))
        wg, wu, wd = (w[l].astype(BF16) for w in (w_ffn_gate, w_ffn_up, w_ffn_down))
        norm_args = (g_post_mix[l], gt1, g_pre_ffn[l], sh2, sc2, wg, wu, wd, g_post_ffn[l], gt2)
        if l % 2 == 0:
            e = l // 2
            w_in = w_in_ab[e].astype(BF16)
            w_out = w_out_ab[e].astype(BF16)
            splits = ((0, lru_w, 1.0), (lru_w, 2 * lru_w, 1.0), (2 * lru_w, w_in.shape[1], 1.0))
            gates = [(_block_diag_halves(0.5 * lru_w_a[e, dr]), _block_diag_halves(0.5 * lru_w_i[e, dr]))
                     for dr in range(2)]

            def mixer(tokens, t_len, rows_per_cond, cond_base, h0):
                u, g, f = _inproj(tokens, g_pre_mix[l], sh1, sc1, w_in, splits, (F32, F32, F32),
                                  rows_per_cond, cond_base, slabbed=(2,))
                u, g = (a.reshape(bsz, t_len, -1) for a in (u, g))
                f = f.reshape(f.shape[0], bsz, t_len, LANES)
                lru = functools.partial(_lru_direction, u, cw=conv_w[e], cb=conv_b[e])
                hf, end_f = lru(h0[0], wa_bd=gates[0][0], wi_bd=gates[0][1], ba=0.5 * lru_b_a[e, 0],
                                bi=0.5 * lru_b_i[e, 0], lam=lru_lam[e, 0], reverse=False)
                lr, end_b = lru(h0[1], wa_bd=gates[1][0], wi_bd=gates[1][1], ba=0.5 * lru_b_a[e, 1],
                                bi=0.5 * lru_b_i[e, 1], lam=lru_lam[e, 1], reverse=True, other=hf, gate=g)
                fo = _fourier(f, f.shape[0] * LANES // FFT_GROUPS)
                lhs = [lr.reshape(tokens.shape[0], -1), fo.reshape(tokens.shape[0], -1)]
                return lhs, (end_f, end_b)

            zeros = jnp.zeros((bsz, 8, lru_w), F32)
            lhs_c, ends = mixer(cs, ctx_len, bsz * ctx_len, ctx_cond, (zeros, zeros))
            lhs_x, _ = mixer(xs, seq, seq, 0, ends)
            if not last:
                cs = _tail(lhs_c, w_out, cs, *norm_args, bsz * ctx_len, ctx_cond)
            xs = _tail(lhs_x, w_out, xs, *norm_args, seq, 0)
        else:
            o = l // 2
            w_qkv = w_qkv_na[o].astype(BF16)
            n_heads = rpb_na.shape[1]
            scale = LOG2E * (d // n_heads) ** -0.5
            q_split, v_split = (0, d, scale), (2 * d, 3 * d, 1.0)
            w_k_t = w_qkv_na[o][:, d:2 * d].T.astype(BF16)
            q, v, kt = _inproj(xs, g_pre_mix[l], sh1, sc1, w_qkv, (q_split, v_split), (BF16, BF16),
                               seq, 0, wt=w_k_t)
            vc, kct = _inproj(cs, g_pre_mix[l], sh1, sc1, w_qkv, (v_split,), (BF16,),
                              bsz * ctx_len, ctx_cond, wt=w_k_t)
            grid = (bsz, seq // GRID_W, GRID_W, d)
            att = _natten(q.reshape(grid), kt, v.reshape(grid), kct, vc.reshape(bsz, ctx_len, d),
                          _natten_bias_table(rpb_na[o]))
            if not last:
                raise NotImplementedError("context update after an attention layer")
            xs = _tail([att.reshape(bsz * seq, d)], w_out_na[o].astype(BF16), xs, *norm_args, seq, 0)
    return xs.reshape(bsz, seq, d)
```

```python
import functools
import math

import numpy as np
import jax
import jax.numpy as jnp
from jax import lax
from jax.experimental import pallas as pl
from jax.experimental.pallas import tpu as pltpu

F32 = jnp.float32
BF16 = jnp.bfloat16

EPS = 1e-6
LRU_C = 8.0
CONV_LEFT = 2
FFT_GROUPS = 8
GRID_W = 64
NA_KH = 8
NA_KW = 16
DFT_RADIX = 8
NEG = -1e30
LOG2E = math.log2(math.e)

VMEM_LIMIT_BYTES = 56 * 1024 * 1024
ROW_TILE = 1024
LRU_TILE = 512
LRU_BATCH = 2
TAIL_TILE = 1024
TAIL_F_CHUNK = 256
LANES = 128
MXU_DIM = 256


def _cparams(*sem):
    return pltpu.CompilerParams(dimension_semantics=sem, vmem_limit_bytes=VMEM_LIMIT_BYTES)


def _const_spec(shape):
    nd = len(shape)
    return pl.BlockSpec(shape, lambda *_: (0,) * nd, pipeline_mode=pl.Buffered(1))


def _sigmoid(x):
    return 0.5 * (1.0 + jnp.tanh(0.5 * x))


def _silu(x):
    return x * _sigmoid(x)


def _gelu_tanh(x):
    c = math.sqrt(2.0 / math.pi)
    return 0.5 * x * (1.0 + jnp.tanh(c * (x + 0.044715 * (x * x * x))))


def _rms(x, g):
    return x * lax.rsqrt(jnp.mean(x * x, axis=-1, keepdims=True) + EPS) * g


def _mod_kernel(cond_ref, w_ref, b_ref, o_ref):
    act = _silu(cond_ref[...]).astype(BF16)
    o_ref[0] = jnp.dot(act, w_ref[0].astype(BF16), preferred_element_type=F32) + b_ref[0]


def _mod_vectors(cond, w_mod, b_mod):
    depth, d, n = w_mod.shape
    tn = n // 4
    return pl.pallas_call(
        _mod_kernel,
        grid=(depth, n // tn),
        in_specs=[pl.BlockSpec(cond.shape, lambda l, j: (0, 0)),
                  pl.BlockSpec((1, d, tn), lambda l, j: (l, 0, j)),
                  pl.BlockSpec((1, 1, tn), lambda l, j: (l, 0, j))],
        out_specs=pl.BlockSpec((1, cond.shape[0], tn), lambda l, j: (l, 0, j)),
        out_shape=jax.ShapeDtypeStruct((depth, cond.shape[0], n), F32),
        compiler_params=_cparams("parallel", "parallel"),
        name="mod",
    )(cond, w_mod, b_mod.reshape(depth, 1, n))


def _transpose_kernel(x_ref, o_ref):
    o_ref[...] = x_ref[...].T.astype(o_ref.dtype)


def _transposed_block(w, layer, col_block, width):
    k = w.shape[1]
    return pl.pallas_call(
        _transpose_kernel,
        grid=(1,),
        in_specs=[pl.BlockSpec((None, k, width), lambda i: (layer, 0, col_block))],
        out_specs=pl.BlockSpec((width, k), lambda i: (0, 0)),
        out_shape=jax.ShapeDtypeStruct((width, k), BF16),
        compiler_params=_cparams("arbitrary"),
        name="transpose",
    )(w)


def _inproj_kernel(x_ref, g_ref, sh_ref, sc_ref, w_ref, *rest, splits, transposed):
    o_refs = rest
    if transposed:
        wt_ref, o_refs, ot_ref = rest[0], rest[1:-1], rest[-1]
    hm = x_ref.shape[0] // 2

    def norm(rs):
        h = _rms(x_ref[rs, :], g_ref[...]) * (1.0 + sc_ref[0]) + sh_ref[0]
        return h.astype(BF16)

    def project(rs, hb):
        if transposed:
            ot_ref[:, rs] = lax.dot_general(wt_ref[...], hb, (((1,), (1,)), ((), ())),
                                            preferred_element_type=F32).astype(ot_ref.dtype)
        for o_ref, (lo, hi, scale) in zip(o_refs, splits):
            z = jnp.dot(hb, w_ref[:, lo:hi], preferred_element_type=F32)
            if scale != 1.0:
                z = z * scale
            if len(o_ref.shape) == 3:
                for j in range(o_ref.shape[0]):
                    o_ref[j, rs, :] = z[:, j * LANES:(j + 1) * LANES].astype(o_ref.dtype)
            else:
                o_ref[rs, :] = z.astype(o_ref.dtype)

    ra, rb = slice(0, hm), slice(hm, 2 * hm)
    ha = norm(ra)
    hb = norm(rb)
    project(ra, ha)
    project(rb, hb)


def _inproj(x2, g, sh, sc, w, splits, dtypes, rows_per_cond, cond_base, slabbed=(), wt=None):
    n, d = x2.shape
    tm = min(ROW_TILE, n)
    tiles_per_cond = rows_per_cond // tm
    cond_map = lambda i: (cond_base + i // tiles_per_cond, 0, 0)
    out_specs, out_shape = [], []
    for j, ((lo, hi, _), dt) in enumerate(zip(splits, dtypes)):
        if j in slabbed:
            ns = (hi - lo) // LANES
            out_specs.append(pl.BlockSpec((ns, tm, LANES), lambda i: (0, i, 0)))
            out_shape.append(jax.ShapeDtypeStruct((ns, n, LANES), dt))
        else:
            out_specs.append(pl.BlockSpec((tm, hi - lo), lambda i: (i, 0)))
            out_shape.append(jax.ShapeDtypeStruct((n, hi - lo), dt))
    in_specs = [pl.BlockSpec((tm, d), lambda i: (i, 0)),
                _const_spec((1, d)),
                pl.BlockSpec((1, 1, d), cond_map),
                pl.BlockSpec((1, 1, d), cond_map),
                _const_spec(w.shape)]
    args = [x2, g.reshape(1, d), sh, sc, w]
    if wt is not None:
        in_specs.append(_const_spec(wt.shape))
        args.append(wt)
        out_specs.append(pl.BlockSpec((wt.shape[0], tm), lambda i: (0, i)))
        out_shape.append(jax.ShapeDtypeStruct((wt.shape[0], n), BF16))
    return pl.pallas_call(
        functools.partial(_inproj_kernel, splits=splits, transposed=wt is not None),
        grid=(n // tm,),
        in_specs=in_specs,
        out_specs=out_specs,
        out_shape=out_shape,
        compiler_params=_cparams("parallel"),
        name="inproj",
    )(*args)


def _lru_kernel(*refs, reverse, combine, tt, nt):
    if combine:
        (up_ref, u_ref, un_ref, h0_ref, cw_ref, cb_ref, wa_ref, wi_ref, ba_ref, bi_ref, lam_ref,
         ro_ref, g_ref, out_ref, hl_ref, ext_s, a_s, b_s, h_s) = refs
    else:
        (up_ref, u_ref, un_ref, h0_ref, cw_ref, cb_ref, wa_ref, wi_ref, ba_ref, bi_ref, lam_ref,
         out_ref, hl_ref, ext_s, a_s, b_s, h_s) = refs
    step = pl.program_id(1)
    tile = nt - 1 - step if reverse else step
    nb, _, c = u_ref.shape
    half = c // 2

    @pl.when(step == 0)
    def _():
        for bi in range(nb):
            h_s[bi] = h0_ref[bi, 0:1, :]

    lam = lam_ref[...]
    softplus_neg_lam = jnp.maximum(-lam, 0.0) + jnp.log1p(jnp.exp(-jnp.abs(lam)))
    coef = (-0.5 * LRU_C) * softplus_neg_lam
    for bi in range(nb):
        ext_s[bi, 0:8, :] = jnp.where(tile > 0, up_ref[bi], 0.0)
        ext_s[bi, 8:8 + tt, :] = u_ref[bi]
        ext_s[bi, 8 + tt:16 + tt, :] = jnp.where(tile < nt - 1, un_ref[bi], 0.0)
        ext = ext_s[bi]
        uc = cb_ref[...]
        for k in range(cw_ref.shape[0]):
            back = (CONV_LEFT - k) % (tt + 16)
            tap = ext if back == 0 else pltpu.roll(ext, back, 0)
            uc = uc + tap[8:8 + tt, :] * cw_ref[k:k + 1, :]

        ub = uc.astype(BF16)
        for hb in range(2):
            sl = slice(hb * half, (hb + 1) * half)
            tr = jnp.tanh(jnp.dot(ub[:, sl], wa_ref[hb], preferred_element_type=F32) + ba_ref[:, sl])
            ti = jnp.tanh(jnp.dot(ub[:, sl], wi_ref[hb], preferred_element_type=F32) + bi_ref[:, sl])
            log_a = coef[:, sl] * tr + coef[:, sl]
            a = jnp.exp(log_a)
            a_s[bi, :, sl] = a
            uh = 0.5 * uc[:, sl]
            w = -jnp.tanh(log_a) * (a * a + 1.0)
            root = jnp.where(w > 0.0, w * lax.rsqrt(w), 0.0)
            b_s[bi, :, sl] = root * (uh * ti + uh)

    row = lax.broadcasted_iota(jnp.int32, (8, c), 0)
    ng = tt // 8

    def group(jj, hs):
        j = ng - 1 - jj if reverse else jj
        off = pl.multiple_of(j * 8, 8)
        out = []
        for bi in range(nb):
            a = a_s[bi, pl.ds(off, 8), :]
            b = b_s[bi, pl.ds(off, 8), :]
            for k in (1, 2):
                shift = 8 - k if reverse else k
                keep = (row < 8 - k) if reverse else (row >= k)
                a_sh = pltpu.roll(a, shift, 0)
                b_sh = pltpu.roll(b, shift, 0)
                b = jnp.where(keep, a * b_sh + b, b)
                a = jnp.where(keep, a * a_sh, a)
            near = b + a * hs[bi]
            far = a * pltpu.roll(near, 4, 0) + b
            hj = jnp.where((row >= 4) if reverse else (row < 4), near, far)
            b_s[bi, pl.ds(off, 8), :] = hj
            out.append(hj[0:1, :] if reverse else hj[7:8, :])
        return tuple(out)

    h_last = lax.fori_loop(0, ng, group, tuple(h_s[bi] for bi in range(nb)), unroll=4)
    for bi in range(nb):
        h_s[bi] = h_last[bi]
        hl_ref[bi] = jnp.broadcast_to(h_last[bi], hl_ref.shape[1:])
        if combine:
            out_ref[bi] = ((ro_ref[bi] + b_s[bi]) * _gelu_tanh(g_ref[bi])).astype(out_ref.dtype)
        else:
            out_ref[bi] = b_s[bi]


def _lru_direction(u, h0, cw, cb, wa_bd, wi_bd, ba, bi, lam, reverse, other=None, gate=None):
    bsz, t, c = u.shape
    tt = min(LRU_TILE, t)
    nt = t // tt
    g8 = tt // 8
    nb = LRU_BATCH if bsz % LRU_BATCH == 0 else 1
    combine = other is not None
    tile = (lambda s: nt - 1 - s) if reverse else (lambda s: s)
    cur = pl.BlockSpec((nb, tt, c), lambda b, s: (b, tile(s), 0))
    state = pl.BlockSpec((nb, 8, c), lambda b, s: (b, 0, 0))
    in_specs = [pl.BlockSpec((nb, 8, c), lambda b, s: (b, jnp.maximum(tile(s) * g8 - 1, 0), 0)),
                cur,
                pl.BlockSpec((nb, 8, c), lambda b, s: (b, jnp.minimum((tile(s) + 1) * g8, t // 8 - 1), 0)),
                state,
                _const_spec(cw.shape), _const_spec((1, c)),
                _const_spec(wa_bd.shape), _const_spec(wi_bd.shape),
                _const_spec((1, c)), _const_spec((1, c)), _const_spec((1, c))]
    args = [u, u, u, h0, cw, cb.reshape(1, c), wa_bd, wi_bd, ba.reshape(1, c), bi.reshape(1, c),
            lam.reshape(1, c)]
    if combine:
        in_specs += [cur, cur]
        args += [other, gate]
    return pl.pallas_call(
        functools.partial(_lru_kernel, reverse=reverse, combine=combine, tt=tt, nt=nt),
        grid=(bsz // nb, nt),
        in_specs=in_specs,
        out_specs=[cur, state],
        out_shape=[jax.ShapeDtypeStruct((bsz, t, c), BF16 if combine else F32),
                   jax.ShapeDtypeStruct((bsz, 8, c), F32)],
        scratch_shapes=[pltpu.VMEM((nb, tt + 16, c), F32), pltpu.VMEM((nb, tt, c), F32),
                        pltpu.VMEM((nb, tt, c), F32), pltpu.VMEM((nb, 1, c), F32)],
        compiler_params=_cparams("parallel", "arbitrary"),
        name="lru_bwd" if reverse else "lru_fwd",
    )(*args)


def _block_diag_halves(w):
    h, d, _ = w.shape
    per = h // 2
    eye = jnp.eye(per, dtype=w.dtype)
    wh = w.reshape(2, per, d, d)
    bd = jnp.einsum("pq,cpij->cpiqj", eye, wh).reshape(2, per * d, per * d)
    return bd.astype(BF16)


def _fourier_kernel(x_ref, cs_ref, tc_ref, ts_ref, cc_ref, sc_ref, out_ref, y_s, *, n2):
    radix, _, cw = y_s.shape
    for n1 in range(radix):
        x = jnp.concatenate([x_ref[j, 0, pl.ds(n1, n2, stride=radix), :] for j in range(x_ref.shape[0])],
                            axis=-1)
        y_s[n1] = jnp.dot(cs_ref[...], x.astype(BF16), preferred_element_type=F32)

    def group(g, carry):
        r = pl.multiple_of(g * 8, 8)
        halves = [slice(h * LANES, (h + 1) * LANES) for h in range(cw // LANES)]
        loaded = [[(y_s[n1, pl.ds(r, 8), ls], y_s[n1, pl.ds(n2 + r, 8), ls]) for n1 in range(radix)]
                  for ls in halves]
        for ls, ys_half in zip(halves, loaded):
            zr, zi = [], []
            for n1, (yc, ys) in enumerate(ys_half):
                if n1 == 0:
                    zr.append(yc)
                    zi.append(-ys)
                else:
                    tc = tc_ref[n1, pl.ds(r, 8), :]
                    ts = ts_ref[n1, pl.ds(r, 8), :]
                    zr.append(yc * tc - ys * ts)
                    zi.append(-(yc * ts + ys * tc))
            ar, ai = _dft8(zr, zi)
            for k1 in range(radix):
                y_s[k1, pl.ds(r, 8), ls] = ar[k1]
                y_s[k1, pl.ds(n2 + r, 8), ls] = ai[k1]
        return carry

    lax.fori_loop(0, n2 // 8, group, 0)

    for k1 in range(radix):
        o = (jnp.dot(y_s[k1, 0:n2, :].astype(BF16), cc_ref[...], preferred_element_type=F32)
             + jnp.dot(y_s[k1, n2:2 * n2, :].astype(BF16), sc_ref[...], preferred_element_type=F32))
        out_ref[0, k1 * n2:(k1 + 1) * n2, :] = o.astype(out_ref.dtype)


def _dft8(zr, zi):
    h = math.sqrt(0.5)

    def dft4(r, i):
        s0r, s0i = r[0] + r[2], i[0] + i[2]
        s1r, s1i = r[0] - r[2], i[0] - i[2]
        s2r, s2i = r[1] + r[3], i[1] + i[3]
        s3r, s3i = r[1] - r[3], i[1] - i[3]
        return ([s0r + s2r, s1r + s3i, s0r - s2r, s1r - s3i],
                [s0i + s2i, s1i - s3r, s0i - s2i, s1i + s3r])

    er, ei = dft4(zr[0::2], zi[0::2])
    odr, odi = dft4(zr[1::2], zi[1::2])
    tr = [odr[0], h * (odr[1] + odi[1]), odi[2], h * (odi[3] - odr[3])]
    ti = [odi[0], h * (odi[1] - odr[1]), -odr[2], -h * (odr[3] + odi[3])]
    return ([er[k] + tr[k] for k in range(4)] + [er[k] - tr[k] for k in range(4)],
            [ei[k] + ti[k] for k in range(4)] + [ei[k] - ti[k] for k in range(4)])


def _fourier(f, group_dim):
    ns, bsz, t, _ = f.shape
    c = ns * LANES
    radix = DFT_RADIX
    n2 = t // radix
    cw = MXU_DIM
    nh = c // cw
    spb = cw // LANES
    k = np.arange(n2)
    ang = 2.0 * np.pi * ((k[:, None] * k[None, :]) % n2) / n2
    cs = np.concatenate([np.cos(ang), np.sin(ang)], 0) / math.sqrt(t)
    n1 = np.arange(radix)
    ang_t = 2.0 * np.pi * ((n1[:, None] * k[None, :]) % t) / t
    tw_shape = (radix, n2, LANES)
    j = np.arange(cw)
    same = (j[:, None] // group_dim) == (j[None, :] // group_dim)
    ang_c = 2.0 * np.pi * ((j[:, None] * j[None, :]) % group_dim) / group_dim
    ccm = np.where(same, np.cos(ang_c), 0.0) / math.sqrt(group_dim)
    scm = np.where(same, np.sin(ang_c), 0.0) / math.sqrt(group_dim)
    assert radix == 8, "_dft8 is the radix stage"
    return pl.pallas_call(
        functools.partial(_fourier_kernel, n2=n2),
        grid=(bsz, nh),
        in_specs=[pl.BlockSpec((spb, 1, t, LANES), lambda b, h: (h, b, 0, 0)),
                  _const_spec((2 * n2, n2)),
                  _const_spec(tw_shape), _const_spec(tw_shape),
                  _const_spec((cw, cw)), _const_spec((cw, cw))],
        out_specs=pl.BlockSpec((1, t, cw), lambda b, h: (b, 0, h), pipeline_mode=pl.Buffered(1)),
        out_shape=jax.ShapeDtypeStruct((bsz, t, c), BF16),
        scratch_shapes=[pltpu.VMEM((radix, 2 * n2, cw), F32)],
        compiler_params=_cparams("parallel", "parallel"),
        name="fourier",
    )(f,
      jnp.asarray(cs, F32).astype(BF16),
      jnp.broadcast_to(jnp.asarray(np.cos(ang_t), F32)[:, :, None], tw_shape),
      jnp.broadcast_to(jnp.asarray(np.sin(ang_t), F32)[:, :, None], tw_shape),
      jnp.asarray(ccm, F32).astype(BF16), jnp.asarray(scm, F32).astype(BF16))


def _natten_kernel(q_ref, kt_ref, v_ref, kct_ref, vc_ref, tz_ref, o_ref, bias_s, *, qr_n, kh):
    rows, w, lanes = q_ref.shape[1:]
    nrb = rows // qr_n
    hd = lanes // 2
    half_n = qr_n // 2
    band = half_n + kh
    nq = half_n * w
    lane = lax.broadcasted_iota(jnp.int32, (1, lanes), 1)
    first = lane < hd
    ones = jnp.ones((1, lanes), BF16)
    mask = [(lane // hd == hh).astype(F32).astype(BF16) for hh in range(2)]
    vc = vc_ref[0]
    vc1 = jnp.concatenate([vc, jnp.broadcast_to(ones, vc.shape)], axis=1)

    def band_start(r_first):
        return jnp.clip(r_first - kh // 2, 0, rows - band)

    geometry_blocks = (0, 1, nrb - 1)

    def geometry(rb):
        return jnp.where(rb == 0, 0, jnp.where(rb == nrb - 1, 2, 1))

    def build_bias(geom):
        rb = geometry_blocks[geom]
        for half in range(2):
            r_first = rb * qr_n + half * half_n
            kr0 = min(max(r_first - kh // 2, 0), rows - band)
            for qr in range(half_n):
                r = r_first + qr
                rs = min(max(r - kh // 2, 0), rows - kh)
                for m in range(band // 2):
                    kr = kr0 + 2 * m
                    in_window = [rs <= kr + j < rs + kh for j in range(2)]
                    pair = kr - r + NA_KH
                    for hh in range(2):
                        row0 = hh * nq + qr * w
                        if not any(in_window):
                            tile = jnp.full((w, lanes), NEG, F32)
                        elif all(in_window):
                            tile = tz_ref[hh, pair]
                        else:
                            keep = (lane < hd) if in_window[0] else (lane >= hd)
                            tile = jnp.where(keep, tz_ref[hh, pair], NEG)
                        bias_s[geom, half, row0:row0 + w, m * lanes:(m + 1) * lanes] = tile

    def scores(rb, half):
        r_first = rb * qr_n + half * half_n
        kr0 = band_start(r_first)
        q = q_ref[0, pl.ds(r_first, half_n)].reshape(nq, lanes)
        qq = jnp.concatenate([q * mask[0], q * mask[1]], axis=0)
        kt = kt_ref[:, pl.ds(pl.multiple_of(kr0 * w, MXU_DIM), band * w)]
        s_loc = jnp.dot(qq, kt, preferred_element_type=F32) + bias_s[geometry(rb), half]
        s_ctx = jnp.dot(qq, kct_ref[...], preferred_element_type=F32)
        return s_loc, s_ctx

    def softmax(s):
        s_loc, s_ctx = s
        mx = jnp.maximum(jnp.max(s_loc, axis=-1, keepdims=True), jnp.max(s_ctx, axis=-1, keepdims=True))
        return jnp.exp2((s_loc - mx).astype(BF16)), jnp.exp2((s_ctx - mx).astype(BF16))

    def attend(rb, half, p):
        r_first = rb * qr_n + half * half_n
        vb = v_ref[0, pl.ds(band_start(r_first), band)].reshape(band * w, lanes)
        vb1 = jnp.concatenate([vb, jnp.broadcast_to(ones, vb.shape)], axis=1)
        acc = (jnp.dot(p[0], vb1, preferred_element_type=F32)
               + jnp.dot(p[1], vc1, preferred_element_type=F32))
        o0 = acc[:nq, :lanes] / acc[:nq, lanes:]
        o1 = acc[nq:, :lanes] / acc[nq:, lanes:]
        o_ref[0, pl.ds(r_first, half_n)] = jnp.where(first, o0, o1).reshape(
            half_n, w, lanes).astype(o_ref.dtype)

    @pl.when(pl.program_id(1) == 0)
    def _():
        for geom in range(len(geometry_blocks)):
            build_bias(geom)

    def row_block(rb, carry):
        s_a = scores(rb, 0)
        s_b = scores(rb, 1)
        p_a = softmax(s_a)
        attend(rb, 0, p_a)
        p_b = softmax(s_b)
        attend(rb, 1, p_b)
        return carry

    lax.fori_loop(0, nrb, row_block, 0, unroll=2)


def _natten_bias_table(rpb):
    n_off = rpb.shape[1]
    n_dc = rpb.shape[2]
    w = GRID_W
    qc = np.arange(w)
    cs = np.clip(qc - NA_KW // 2, 0, w - NA_KW)
    kc = np.arange(w)
    inside = (kc[None, :] >= cs[:, None]) & (kc[None, :] < cs[:, None] + NA_KW)
    dc = kc[None, :] - qc[:, None] + NA_KW - 1
    pick = (np.arange(n_dc)[:, None, None] == dc[None]) & inside[None]
    pick2 = np.zeros((2 * n_dc, w, 2 * w), np.float32)
    pick2[:n_dc, :, :w] = pick
    pick2[n_dc:, :, w:] = pick
    ext = jnp.pad(LOG2E * rpb, ((0, 0), (1, 1), (0, 0)))
    rows2 = jnp.concatenate([ext[:, :-1], ext[:, 1:]], axis=-1)
    table = jnp.einsum("hed,dqj->heqj", rows2, jnp.asarray(pick2), precision=lax.Precision.HIGHEST)
    slab_ok = np.arange(n_off + 2) - 1
    slab_ok = (slab_ok >= 0) & (slab_ok < n_off)
    ok = np.concatenate([slab_ok[:-1, None, None] & inside[None], slab_ok[1:, None, None] & inside[None]],
                        axis=-1)
    return table + jnp.asarray(np.where(ok, 0.0, NEG), F32)


def _natten(q, kt, v, kct, vc, table):
    bsz, rows, w, d = q.shape
    l = vc.shape[1]
    qr_n = NA_KH
    kh = min(NA_KH, rows)
    band = qr_n // 2 + kh
    assert w == GRID_W and 2 * w == LANES and rows % (2 * qr_n) == 0
    assert (qr_n // 2 * w) % MXU_DIM == 0 and l % LANES == 0
    nhp = d // LANES
    image = pl.BlockSpec((1, rows, w, LANES), lambda hp, b: (b, 0, 0, hp))
    return pl.pallas_call(
        functools.partial(_natten_kernel, qr_n=qr_n, kh=kh),
        grid=(nhp, bsz),
        in_specs=[image,
                  pl.BlockSpec((LANES, rows * w), lambda hp, b: (hp, b)),
                  image,
                  pl.BlockSpec((LANES, l), lambda hp, b: (hp, b)),
                  pl.BlockSpec((1, l, LANES), lambda hp, b: (b, 0, hp)),
                  pl.BlockSpec((2,) + table.shape[1:], lambda hp, b: (hp, 0, 0, 0))],
        out_specs=image,
        out_shape=jax.ShapeDtypeStruct(q.shape, BF16),
        scratch_shapes=[pltpu.VMEM((3, 2, qr_n * w, band * w), F32)],
        compiler_params=_cparams("parallel", "arbitrary"),
        name="natten",
    )(q, kt, v, kct, vc, table)


def _tail_kernel(*refs, n_lhs, f_chunk):
    lhs = refs[:n_lhs]
    (wo_ref, x_ref, gpm_ref, gt1_ref, gpf_ref, sh2_ref, sc2_ref, wg_ref, wu_ref, wd_ref, gpo_ref,
     gt2_ref, o_ref) = refs[n_lhs:]
    hm = x_ref.shape[0] // 2

    def outproj(rs):
        y, k0 = None, 0
        for a_ref in lhs:
            k1 = k0 + a_ref.shape[1]
            part = jnp.dot(a_ref[rs, :], wo_ref[k0:k1, :], preferred_element_type=F32)
            y = part if y is None else y + part
            k0 = k1
        return y

    def norms(rs, y):
        x1 = x_ref[rs, :] + gt1_ref[0] * _rms(y, gpm_ref[...])
        fx = (_rms(x1, gpf_ref[...]) * (1.0 + sc2_ref[0]) + sh2_ref[0]).astype(BF16)
        return x1, fx

    def ffn(fx):
        acc = None
        for lo in range(0, wg_ref.shape[1], f_chunk):
            gate = jnp.dot(fx, wg_ref[:, lo:lo + f_chunk], preferred_element_type=F32)
            up = jnp.dot(fx, wu_ref[:, lo:lo + f_chunk], preferred_element_type=F32)
            hid = (_silu(gate) * up).astype(BF16)
            part = jnp.dot(hid, wd_ref[lo:lo + f_chunk, :], preferred_element_type=F32)
            acc = part if acc is None else acc + part
        return acc

    ra, rb = slice(0, hm), slice(hm, 2 * hm)
    ya = outproj(ra)
    yb = outproj(rb)
    x1a, fxa = norms(ra, ya)
    fa = ffn(fxa)
    x1b, fxb = norms(rb, yb)
    o_ref[ra, :] = x1a + gt2_ref[0] * _rms(fa, gpo_ref[...])
    fb = ffn(fxb)
    o_ref[rb, :] = x1b + gt2_ref[0] * _rms(fb, gpo_ref[...])


def _tail(lhs, wo, x2, gpm, gt1, gpf, sh2, sc2, layer, wg, wu, wd, gpo, gt2, rows_per_cond, cond_base):
    n, d = x2.shape
    f = wg.shape[2]
    tm = min(TAIL_TILE, n)
    assert f % TAIL_F_CHUNK == 0
    tiles_per_cond = rows_per_cond // tm
    cond_map = lambda i: (cond_base + i // tiles_per_cond, 0, 0)
    cond = pl.BlockSpec((1, 1, d), cond_map)
    vec = _const_spec((1, d))

    def layer_spec(w):
        return pl.BlockSpec((None,) + w.shape[1:], lambda i: (layer, 0, 0), pipeline_mode=pl.Buffered(1))
    return pl.pallas_call(
        functools.partial(_tail_kernel, n_lhs=len(lhs), f_chunk=TAIL_F_CHUNK),
        grid=(n // tm,),
        in_specs=([pl.BlockSpec((tm, a.shape[1]), lambda i: (i, 0)) for a in lhs]
                  + [_const_spec(wo.shape),
                     pl.BlockSpec((tm, d), lambda i: (i, 0)), vec, cond, vec, cond, cond,
                     layer_spec(wg), layer_spec(wu), layer_spec(wd), vec, cond]),
        out_specs=pl.BlockSpec((tm, d), lambda i: (i, 0)),
        out_shape=jax.ShapeDtypeStruct((n, d), F32),
        compiler_params=_cparams("parallel"),
        name="tail",
    )(*lhs, wo, x2, gpm.reshape(1, d), gt1, gpf.reshape(1, d), sh2, sc2, wg, wu, wd,
      gpo.reshape(1, d), gt2)


def kernel(x, c, ctx, c_ctx, w_mod, b_mod, g_pre_mix, g_post_mix, g_pre_ffn, g_post_ffn, w_ffn_gate, w_ffn_up, w_ffn_down, w_in_ab, conv_w, conv_b, lru_w_a, lru_b_a, lru_w_i, lru_b_i, lru_lam, w_out_ab, w_qkv_na, rpb_na, w_out_na):
    bsz, seq, d = x.shape
    ctx_len = ctx.shape[1]
    depth = w_mod.shape[0]
    lru_w = conv_w.shape[-1]
    n_cond = 8
    assert bsz < n_cond
    ctx_cond = bsz

    cond = jnp.concatenate([c, c_ctx[None], jnp.zeros((n_cond - bsz - 1, d), F32)], 0)
    mod = _mod_vectors(cond, w_mod, b_mod).reshape(depth, n_cond, 6, 1, d)

    xs = x.reshape(bsz * seq, d)
    cs = ctx.reshape(bsz * ctx_len, d)
    wg, wu, wd = (w.astype(BF16) for w in (w_ffn_gate, w_ffn_up, w_ffn_down))
    for l in range(depth):
        last = l == depth - 1
        sh1, sc1, gt1, sh2, sc2, gt2 = (mod[l, :, j] for j in range(6))
        norm_args = (g_post_mix[l], gt1, g_pre_ffn[l], sh2, sc2, l, wg, wu, wd, g_post_ffn[l], gt2)
        if l % 2 == 0:
            e = l // 2
            w_in = w_in_ab[e].astype(BF16)
            w_out = w_out_ab[e].astype(BF16)
            splits = ((0, lru_w, 1.0), (lru_w, 2 * lru_w, 1.0), (2 * lru_w, w_in.shape[1], 1.0))
            gates = [(_block_diag_halves(0.5 * lru_w_a[e, dr]), _block_diag_halves(0.5 * lru_w_i[e, dr]))
                     for dr in range(2)]

            def mixer(tokens, t_len, rows_per_cond, cond_base, h0):
                u, g, f = _inproj(tokens, g_pre_mix[l], sh1, sc1, w_in, splits, (F32, F32, F32),
                                  rows_per_cond, cond_base, slabbed=(2,))
                u, g = (a.reshape(bsz, t_len, -1) for a in (u, g))
                f = f.reshape(f.shape[0], bsz, t_len, LANES)
                lru = functools.partial(_lru_direction, u, cw=conv_w[e], cb=conv_b[e])
                hf, end_f = lru(h0[0], wa_bd=gates[0][0], wi_bd=gates[0][1], ba=0.5 * lru_b_a[e, 0],
                                bi=0.5 * lru_b_i[e, 0], lam=lru_lam[e, 0], reverse=False)
                lr, end_b = lru(h0[1], wa_bd=gates[1][0], wi_bd=gates[1][1], ba=0.5 * lru_b_a[e, 1],
                                bi=0.5 * lru_b_i[e, 1], lam=lru_lam[e, 1], reverse=True, other=hf, gate=g)
                fo = _fourier(f, f.shape[0] * LANES // FFT_GROUPS)
                lhs = [lr.reshape(tokens.shape[0], -1), fo.reshape(tokens.shape[0], -1)]
                return lhs, (end_f, end_b)

            zeros = jnp.zeros((bsz, 8, lru_w), F32)
            lhs_c, ends = mixer(cs, ctx_len, bsz * ctx_len, ctx_cond, (zeros, zeros))
            lhs_x, _ = mixer(xs, seq, seq, 0, ends)
            if not last:
                cs = _tail(lhs_c, w_out, cs, *norm_args, bsz * ctx_len, ctx_cond)
            xs = _tail(lhs_x, w_out, xs, *norm_args, seq, 0)
        else:
            o = l // 2
            w_qkv = w_qkv_na[o].astype(BF16)
            n_heads = rpb_na.shape[1]
            scale = LOG2E * (d // n_heads) ** -0.5
            q_split, v_split = (0, d, scale), (2 * d, 3 * d, 1.0)
            w_k_t = _transposed_block(w_qkv_na, o, 1, d)
            q, v, kt = _inproj(xs, g_pre_mix[l], sh1, sc1, w_qkv, (q_split, v_split), (BF16, BF16),
                               seq, 0, wt=w_k_t)
            vc, kct = _inproj(cs, g_pre_mix[l], sh1, sc1, w_qkv, (v_split,), (BF16,),
                              bsz * ctx_len, ctx_cond, wt=w_k_t)
            grid = (bsz, seq // GRID_W, GRID_W, d)
            att = _natten(q.reshape(grid), kt, v.reshape(grid), kct, vc.reshape(bsz, ctx_len, d),
                          _natten_bias_table(rpb_na[o]))
            if not last:
                raise NotImplementedError("context update after an attention layer")
            xs = _tail([att.reshape(bsz * seq, d)], w_out_na[o].astype(BF16), xs, *norm_args, seq, 0)
    return xs.reshape(bsz, seq, d)
```

```python
import functools
import math

import numpy as np
import jax
import jax.numpy as jnp
from jax import lax
from jax.experimental import pallas as pl
from jax.experimental.pallas import tpu as pltpu

F32 = jnp.float32
BF16 = jnp.bfloat16

EPS = 1e-6
LRU_C = 8.0
CONV_LEFT = 2
FFT_GROUPS = 8
GRID_W = 64
NA_KH = 8
NA_KW = 16
DFT_RADIX = 8
NEG = -1e30
LOG2E = math.log2(math.e)

VMEM_LIMIT_BYTES = 56 * 1024 * 1024
ROW_TILE = 1024
LRU_TILE = 512
LRU_BATCH = 2
TAIL_TILE = 1024
TAIL_F_CHUNK = 256
LANES = 128
MXU_DIM = 256


def _cparams(*sem):
    return pltpu.CompilerParams(dimension_semantics=sem, vmem_limit_bytes=VMEM_LIMIT_BYTES)


def _const_spec(shape):
    nd = len(shape)
    return pl.BlockSpec(shape, lambda *_: (0,) * nd, pipeline_mode=pl.Buffered(1))


def _sigmoid(x):
    return 0.5 * (1.0 + jnp.tanh(0.5 * x))


def _silu(x):
    return x * _sigmoid(x)


def _gelu_tanh(x):
    c = math.sqrt(2.0 / math.pi)
    return 0.5 * x * (1.0 + jnp.tanh(c * (x + 0.044715 * (x * x * x))))


def _rms(x, g):
    return x * lax.rsqrt(jnp.mean(x * x, axis=-1, keepdims=True) + EPS) * g


def _mod_kernel(cond_ref, w_ref, b_ref, o_ref):
    act = _silu(cond_ref[...]).astype(BF16)
    o_ref[0] = jnp.dot(act, w_ref[0].astype(BF16), preferred_element_type=F32) + b_ref[0]


def _mod_vectors(cond, w_mod, b_mod):
    depth, d, n = w_mod.shape
    tn = n // 4
    return pl.pallas_call(
        _mod_kernel,
        grid=(depth, n // tn),
        in_specs=[pl.BlockSpec(cond.shape, lambda l, j: (0, 0)),
                  pl.BlockSpec((1, d, tn), lambda l, j: (l, 0, j)),
                  pl.BlockSpec((1, 1, tn), lambda l, j: (l, 0, j))],
        out_specs=pl.BlockSpec((1, cond.shape[0], tn), lambda l, j: (l, 0, j)),
        out_shape=jax.ShapeDtypeStruct((depth, cond.shape[0], n), F32),
        compiler_params=_cparams("parallel", "parallel"),
        name="mod",
    )(cond, w_mod, b_mod.reshape(depth, 1, n))


def _transpose_kernel(x_ref, o_ref):
    o_ref[...] = x_ref[...].T.astype(o_ref.dtype)


def _transposed_block(w, layer, col_block, width):
    k = w.shape[1]
    return pl.pallas_call(
        _transpose_kernel,
        grid=(1,),
        in_specs=[pl.BlockSpec((None, k, width), lambda i: (layer, 0, col_block))],
        out_specs=pl.BlockSpec((width, k), lambda i: (0, 0)),
        out_shape=jax.ShapeDtypeStruct((width, k), BF16),
        compiler_params=_cparams("arbitrary"),
        name="transpose",
    )(w)


def _inproj_kernel(x_ref, g_ref, sh_ref, sc_ref, w_ref, *rest, splits, transposed):
    o_refs = rest
    if transposed:
        wt_ref, o_refs, ot_ref = rest[0], rest[1:-1], rest[-1]
    hm = x_ref.shape[0] // 2

    def norm(rs):
        h = _rms(x_ref[rs, :], g_ref[...]) * (1.0 + sc_ref[0]) + sh_ref[0]
        return h.astype(BF16)

    def project(rs, hb):
        if transposed:
            ot_ref[:, rs] = lax.dot_general(wt_ref[...], hb, (((1,), (1,)), ((), ())),
                                            preferred_element_type=F32).astype(ot_ref.dtype)
        for o_ref, (lo, hi, scale) in zip(o_refs, splits):
            z = jnp.dot(hb, w_ref[:, lo:hi], preferred_element_type=F32)
            if scale != 1.0:
                z = z * scale
            if len(o_ref.shape) == 3:
                for j in range(o_ref.shape[0]):
                    o_ref[j, rs, :] = z[:, j * LANES:(j + 1) * LANES].astype(o_ref.dtype)
            else:
                o_ref[rs, :] = z.astype(o_ref.dtype)

    ra, rb = slice(0, hm), slice(hm, 2 * hm)
    ha = norm(ra)
    hb = norm(rb)
    project(ra, ha)
    project(rb, hb)


def _inproj(x2, g, sh, sc, w, splits, dtypes, rows_per_cond, cond_base, slabbed=(), wt=None):
    n, d = x2.shape
    tm = min(ROW_TILE, n)
    tiles_per_cond = rows_per_cond // tm
    cond_map = lambda i: (cond_base + i // tiles_per_cond, 0, 0)
    out_specs, out_shape = [], []
    for j, ((lo, hi, _), dt) in enumerate(zip(splits, dtypes)):
        if j in slabbed:
            ns = (hi - lo) // LANES
            out_specs.append(pl.BlockSpec((ns, tm, LANES), lambda i: (0, i, 0)))
            out_shape.append(jax.ShapeDtypeStruct((ns, n, LANES), dt))
        else:
            out_specs.append(pl.BlockSpec((tm, hi - lo), lambda i: (i, 0)))
            out_shape.append(jax.ShapeDtypeStruct((n, hi - lo), dt))
    in_specs = [pl.BlockSpec((tm, d), lambda i: (i, 0)),
                _const_spec((1, d)),
                pl.BlockSpec((1, 1, d), cond_map),
                pl.BlockSpec((1, 1, d), cond_map),
                _const_spec(w.shape)]
    args = [x2, g.reshape(1, d), sh, sc, w]
    if wt is not None:
        in_specs.append(_const_spec(wt.shape))
        args.append(wt)
        out_specs.append(pl.BlockSpec((wt.shape[0], tm), lambda i: (0, i)))
        out_shape.append(jax.ShapeDtypeStruct((wt.shape[0], n), BF16))
    return pl.pallas_call(
        functools.partial(_inproj_kernel, splits=splits, transposed=wt is not None),
        grid=(n // tm,),
        in_specs=in_specs,
        out_specs=out_specs,
        out_shape=out_shape,
        compiler_params=_cparams("parallel"),
        name="inproj",
    )(*args)


def _lru_kernel(*refs, reverse, combine, tt, nt):
    if combine:
        (up_ref, u_ref, un_ref, h0_ref, cw_ref, cb_ref, wa_ref, wi_ref, ba_ref, bi_ref, lam_ref,
         ro_ref, g_ref, out_ref, hl_ref, ext_s, a_s, b_s, h_s) = refs
    else:
        (up_ref, u_ref, un_ref, h0_ref, cw_ref, cb_ref, wa_ref, wi_ref, ba_ref, bi_ref, lam_ref,
         out_ref, hl_ref, ext_s, a_s, b_s, h_s) = refs
    step = pl.program_id(1)
    tile = nt - 1 - step if reverse else step
    nb, _, c = u_ref.shape
    half = c // 2

    @pl.when(step == 0)
    def _():
        for bi in range(nb):
            h_s[bi] = h0_ref[bi, 0:1, :]

    lam = lam_ref[...]
    softplus_neg_lam = jnp.maximum(-lam, 0.0) + jnp.log1p(jnp.exp(-jnp.abs(lam)))
    coef = (-0.5 * LRU_C) * softplus_neg_lam
    for bi in range(nb):
        ext_s[bi, 0:8, :] = jnp.where(tile > 0, up_ref[bi], 0.0)
        ext_s[bi, 8:8 + tt, :] = u_ref[bi]
        ext_s[bi, 8 + tt:16 + tt, :] = jnp.where(tile < nt - 1, un_ref[bi], 0.0)
        ext = ext_s[bi]
        uc = cb_ref[...]
        for k in range(cw_ref.shape[0]):
            back = (CONV_LEFT - k) % (tt + 16)
            tap = ext if back == 0 else pltpu.roll(ext, back, 0)
            uc = uc + tap[8:8 + tt, :] * cw_ref[k:k + 1, :]

        ub = uc.astype(BF16)
        for hb in range(2):
            sl = slice(hb * half, (hb + 1) * half)
            tr = jnp.tanh(jnp.dot(ub[:, sl], wa_ref[hb], preferred_element_type=F32) + ba_ref[:, sl])
            ti = jnp.tanh(jnp.dot(ub[:, sl], wi_ref[hb], preferred_element_type=F32) + bi_ref[:, sl])
            log_a = coef[:, sl] * tr + coef[:, sl]
            a = jnp.exp(log_a)
            a_s[bi, :, sl] = a
            uh = 0.5 * uc[:, sl]
            w = -jnp.tanh(log_a) * (a * a + 1.0)
            root = jnp.where(w > 0.0, w * lax.rsqrt(w), 0.0)
            b_s[bi, :, sl] = root * (uh * ti + uh)

    row = lax.broadcasted_iota(jnp.int32, (8, c), 0)
    ng = tt // 8

    def group(jj, hs):
        j = ng - 1 - jj if reverse else jj
        off = pl.multiple_of(j * 8, 8)
        out = []
        for bi in range(nb):
            a = a_s[bi, pl.ds(off, 8), :]
            b = b_s[bi, pl.ds(off, 8), :]
            for k in (1, 2):
                shift = 8 - k if reverse else k
                keep = (row < 8 - k) if reverse else (row >= k)
                a_sh = pltpu.roll(a, shift, 0)
                b_sh = pltpu.roll(b, shift, 0)
                b = jnp.where(keep, a * b_sh + b, b)
                a = jnp.where(keep, a * a_sh, a)
            near = b + a * hs[bi]
            far = a * pltpu.roll(near, 4, 0) + b
            hj = jnp.where((row >= 4) if reverse else (row < 4), near, far)
            b_s[bi, pl.ds(off, 8), :] = hj
            out.append(hj[0:1, :] if reverse else hj[7:8, :])
        return tuple(out)

    h_last = lax.fori_loop(0, ng, group, tuple(h_s[bi] for bi in range(nb)), unroll=8)
    for bi in range(nb):
        h_s[bi] = h_last[bi]
        hl_ref[bi] = jnp.broadcast_to(h_last[bi], hl_ref.shape[1:])
        if combine:
            out_ref[bi] = ((ro_ref[bi] + b_s[bi]) * _gelu_tanh(g_ref[bi])).astype(out_ref.dtype)
        else:
            out_ref[bi] = b_s[bi]


def _lru_direction(u, h0, cw, cb, wa_bd, wi_bd, ba, bi, lam, reverse, other=None, gate=None):
    bsz, t, c = u.shape
    tt = min(LRU_TILE, t)
    nt = t // tt
    g8 = tt // 8
    nb = LRU_BATCH if bsz % LRU_BATCH == 0 else 1
    combine = other is not None
    tile = (lambda s: nt - 1 - s) if reverse else (lambda s: s)
    cur = pl.BlockSpec((nb, tt, c), lambda b, s: (b, tile(s), 0))
    state = pl.BlockSpec((nb, 8, c), lambda b, s: (b, 0, 0))
    in_specs = [pl.BlockSpec((nb, 8, c), lambda b, s: (b, jnp.maximum(tile(s) * g8 - 1, 0), 0)),
                cur,
                pl.BlockSpec((nb, 8, c), lambda b, s: (b, jnp.minimum((tile(s) + 1) * g8, t // 8 - 1), 0)),
                state,
                _const_spec(cw.shape), _const_spec((1, c)),
                _const_spec(wa_bd.shape), _const_spec(wi_bd.shape),
                _const_spec((1, c)), _const_spec((1, c)), _const_spec((1, c))]
    args = [u, u, u, h0, cw, cb.reshape(1, c), wa_bd, wi_bd, ba.reshape(1, c), bi.reshape(1, c),
            lam.reshape(1, c)]
    if combine:
        in_specs += [cur, cur]
        args += [other, gate]
    return pl.pallas_call(
        functools.partial(_lru_kernel, reverse=reverse, combine=combine, tt=tt, nt=nt),
        grid=(bsz // nb, nt),
        in_specs=in_specs,
        out_specs=[cur, state],
        out_shape=[jax.ShapeDtypeStruct((bsz, t, c), BF16 if combine else F32),
                   jax.ShapeDtypeStruct((bsz, 8, c), F32)],
        scratch_shapes=[pltpu.VMEM((nb, tt + 16, c), F32), pltpu.VMEM((nb, tt, c), F32),
                        pltpu.VMEM((nb, tt, c), F32), pltpu.VMEM((nb, 1, c), F32)],
        compiler_params=_cparams("parallel", "arbitrary"),
        name="lru_bwd" if reverse else "lru_fwd",
    )(*args)


def _block_diag_halves(w):
    h, d, _ = w.shape
    per = h // 2
    eye = jnp.eye(per, dtype=w.dtype)
    wh = w.reshape(2, per, d, d)
    bd = jnp.einsum("pq,cpij->cpiqj", eye, wh).reshape(2, per * d, per * d)
    return bd.astype(BF16)


def _fourier_kernel(x_ref, cs_ref, tc_ref, ts_ref, cc_ref, sc_ref, out_ref, y_s, *, n2):
    radix, _, cw = y_s.shape
    for n1 in range(radix):
        x = jnp.concatenate([x_ref[j, 0, pl.ds(n1, n2, stride=radix), :] for j in range(x_ref.shape[0])],
                            axis=-1)
        y_s[n1] = jnp.dot(cs_ref[...], x.astype(BF16), preferred_element_type=F32)

    def group(g, carry):
        r = pl.multiple_of(g * 8, 8)
        halves = [slice(h * LANES, (h + 1) * LANES) for h in range(cw // LANES)]
        loaded = [[(y_s[n1, pl.ds(r, 8), ls], y_s[n1, pl.ds(n2 + r, 8), ls]) for n1 in range(radix)]
                  for ls in halves]
        for ls, ys_half in zip(halves, loaded):
            zr, zi = [], []
            for n1, (yc, ys) in enumerate(ys_half):
                if n1 == 0:
                    zr.append(yc)
                    zi.append(-ys)
                else:
                    tc = tc_ref[n1, pl.ds(r, 8), :]
                    ts = ts_ref[n1, pl.ds(r, 8), :]
                    zr.append(yc * tc - ys * ts)
                    zi.append(-(yc * ts + ys * tc))
            ar, ai = _dft8(zr, zi)
            for k1 in range(radix):
                y_s[k1, pl.ds(r, 8), ls] = ar[k1]
                y_s[k1, pl.ds(n2 + r, 8), ls] = ai[k1]
        return carry

    lax.fori_loop(0, n2 // 8, group, 0, unroll=2)

    for k1 in range(radix):
        o = (jnp.dot(y_s[k1, 0:n2, :].astype(BF16), cc_ref[...], preferred_element_type=F32)
             + jnp.dot(y_s[k1, n2:2 * n2, :].astype(BF16), sc_ref[...], preferred_element_type=F32))
        out_ref[0, k1 * n2:(k1 + 1) * n2, :] = o.astype(out_ref.dtype)


def _dft8(zr, zi):
    h = math.sqrt(0.5)

    def dft4(r, i):
        s0r, s0i = r[0] + r[2], i[0] + i[2]
        s1r, s1i = r[0] - r[2], i[0] - i[2]
        s2r, s2i = r[1] + r[3], i[1] + i[3]
        s3r, s3i = r[1] - r[3], i[1] - i[3]
        return ([s0r + s2r, s1r + s3i, s0r - s2r, s1r - s3i],
                [s0i + s2i, s1i - s3r, s0i - s2i, s1i + s3r])

    er, ei = dft4(zr[0::2], zi[0::2])
    odr, odi = dft4(zr[1::2], zi[1::2])
    tr = [odr[0], h * (odr[1] + odi[1]), odi[2], h * (odi[3] - odr[3])]
    ti = [odi[0], h * (odi[1] - odr[1]), -odr[2], -h * (odr[3] + odi[3])]
    return ([er[k] + tr[k] for k in range(4)] + [er[k] - tr[k] for k in range(4)],
            [ei[k] + ti[k] for k in range(4)] + [ei[k] - ti[k] for k in range(4)])


def _fourier(f, group_dim):
    ns, bsz, t, _ = f.shape
    c = ns * LANES
    radix = DFT_RADIX
    n2 = t // radix
    cw = MXU_DIM
    nh = c // cw
    spb = cw // LANES
    k = np.arange(n2)
    ang = 2.0 * np.pi * ((k[:, None] * k[None, :]) % n2) / n2
    cs = np.concatenate([np.cos(ang), np.sin(ang)], 0) / math.sqrt(t)
    n1 = np.arange(radix)
    ang_t = 2.0 * np.pi * ((n1[:, None] * k[None, :]) % t) / t
    tw_shape = (radix, n2, LANES)
    j = np.arange(cw)
    same = (j[:, None] // group_dim) == (j[None, :] // group_dim)
    ang_c = 2.0 * np.pi * ((j[:, None] * j[None, :]) % group_dim) / group_dim
    ccm = np.where(same, np.cos(ang_c), 0.0) / math.sqrt(group_dim)
    scm = np.where(same, np.sin(ang_c), 0.0) / math.sqrt(group_dim)
    assert radix == 8, "_dft8 is the radix stage"
    return pl.pallas_call(
        functools.partial(_fourier_kernel, n2=n2),
        grid=(bsz, nh),
        in_specs=[pl.BlockSpec((spb, 1, t, LANES), lambda b, h: (h, b, 0, 0)),
                  _const_spec((2 * n2, n2)),
                  _const_spec(tw_shape), _const_spec(tw_shape),
                  _const_spec((cw, cw)), _const_spec((cw, cw))],
        out_specs=pl.BlockSpec((1, t, cw), lambda b, h: (b, 0, h), pipeline_mode=pl.Buffered(1)),
        out_shape=jax.ShapeDtypeStruct((bsz, t, c), BF16),
        scratch_shapes=[pltpu.VMEM((radix, 2 * n2, cw), F32)],
        compiler_params=_cparams("parallel", "parallel"),
        name="fourier",
    )(f,
      jnp.asarray(cs, F32).astype(BF16),
      jnp.broadcast_to(jnp.asarray(np.cos(ang_t), F32)[:, :, None], tw_shape),
      jnp.broadcast_to(jnp.asarray(np.sin(ang_t), F32)[:, :, None], tw_shape),
      jnp.asarray(ccm, F32).astype(BF16), jnp.asarray(scm, F32).astype(BF16))


def _natten_kernel(q_ref, kt_ref, v_ref, kct_ref, vc_ref, tz_ref, o_ref, bias_s, *, qr_n, kh):
    rows, w, lanes = q_ref.shape[1:]
    nrb = rows // qr_n
    hd = lanes // 2
    half_n = qr_n // 2
    band = half_n + kh
    nq = half_n * w
    lane = lax.broadcasted_iota(jnp.int32, (1, lanes), 1)
    first = lane < hd
    ones = jnp.ones((1, lanes), BF16)
    mask = [(lane // hd == hh).astype(F32).astype(BF16) for hh in range(2)]
    vc = vc_ref[0]
    vc1 = jnp.concatenate([vc, jnp.broadcast_to(ones, vc.shape)], axis=1)

    def band_start(r_first):
        return jnp.clip(r_first - kh // 2, 0, rows - band)

    geometry_blocks = (0, 1, nrb - 1)

    def geometry(rb):
        return jnp.where(rb == 0, 0, jnp.where(rb == nrb - 1, 2, 1))

    def build_bias(geom):
        rb = geometry_blocks[geom]
        for half in range(2):
            r_first = rb * qr_n + half * half_n
            kr0 = min(max(r_first - kh // 2, 0), rows - band)
            for qr in range(half_n):
                r = r_first + qr
                rs = min(max(r - kh // 2, 0), rows - kh)
                for m in range(band // 2):
                    kr = kr0 + 2 * m
                    in_window = [rs <= kr + j < rs + kh for j in range(2)]
                    pair = kr - r + NA_KH
                    for hh in range(2):
                        row0 = hh * nq + qr * w
                        if not any(in_window):
                            tile = jnp.full((w, lanes), NEG, F32)
                        elif all(in_window):
                            tile = tz_ref[hh, pair]
                        else:
                            keep = (lane < hd) if in_window[0] else (lane >= hd)
                            tile = jnp.where(keep, tz_ref[hh, pair], NEG)
                        bias_s[geom, half, row0:row0 + w, m * lanes:(m + 1) * lanes] = tile

    def scores(rb, half):
        r_first = rb * qr_n + half * half_n
        kr0 = band_start(r_first)
        q = q_ref[0, pl.ds(r_first, half_n)].reshape(nq, lanes)
        qq = jnp.concatenate([q * mask[0], q * mask[1]], axis=0)
        kt = kt_ref[:, pl.ds(pl.multiple_of(kr0 * w, MXU_DIM), band * w)]
        s_loc = jnp.dot(qq, kt, preferred_element_type=F32) + bias_s[geometry(rb), half]
        s_ctx = jnp.dot(qq, kct_ref[...], preferred_element_type=F32)
        return s_loc, s_ctx

    def softmax(s):
        s_loc, s_ctx = s
        mx = jnp.maximum(jnp.max(s_loc, axis=-1, keepdims=True), jnp.max(s_ctx, axis=-1, keepdims=True))
        return jnp.exp2((s_loc - mx).astype(BF16)), jnp.exp2((s_ctx - mx).astype(BF16))

    def attend(rb, half, p):
        r_first = rb * qr_n + half * half_n
        vb = v_ref[0, pl.ds(band_start(r_first), band)].reshape(band * w, lanes)
        vb1 = jnp.concatenate([vb, jnp.broadcast_to(ones, vb.shape)], axis=1)
        acc = (jnp.dot(p[0], vb1, preferred_element_type=F32)
               + jnp.dot(p[1], vc1, preferred_element_type=F32))
        o0 = acc[:nq, :lanes] / acc[:nq, lanes:]
        o1 = acc[nq:, :lanes] / acc[nq:, lanes:]
        o_ref[0, pl.ds(r_first, half_n)] = jnp.where(first, o0, o1).reshape(
            half_n, w, lanes).astype(o_ref.dtype)

    @pl.when(pl.program_id(1) == 0)
    def _():
        for geom in range(len(geometry_blocks)):
            build_bias(geom)

    def row_block(rb, carry):
        s_a = scores(rb, 0)
        s_b = scores(rb, 1)
        p_a = softmax(s_a)
        attend(rb, 0, p_a)
        p_b = softmax(s_b)
        attend(rb, 1, p_b)
        return carry

    lax.fori_loop(0, nrb, row_block, 0, unroll=2)


def _natten_bias_table(rpb):
    n_off = rpb.shape[1]
    n_dc = rpb.shape[2]
    w = GRID_W
    qc = np.arange(w)
    cs = np.clip(qc - NA_KW // 2, 0, w - NA_KW)
    kc = np.arange(w)
    inside = (kc[None, :] >= cs[:, None]) & (kc[None, :] < cs[:, None] + NA_KW)
    dc = kc[None, :] - qc[:, None] + NA_KW - 1
    pick = (np.arange(n_dc)[:, None, None] == dc[None]) & inside[None]
    pick2 = np.zeros((2 * n_dc, w, 2 * w), np.float32)
    pick2[:n_dc, :, :w] = pick
    pick2[n_dc:, :, w:] = pick
    ext = jnp.pad(LOG2E * rpb, ((0, 0), (1, 1), (0, 0)))
    rows2 = jnp.concatenate([ext[:, :-1], ext[:, 1:]], axis=-1)
    table = jnp.einsum("hed,dqj->heqj", rows2, jnp.asarray(pick2), precision=lax.Precision.HIGHEST)
    slab_ok = np.arange(n_off + 2) - 1
    slab_ok = (slab_ok >= 0) & (slab_ok < n_off)
    ok = np.concatenate([slab_ok[:-1, None, None] & inside[None], slab_ok[1:, None, None] & inside[None]],
                        axis=-1)
    return table + jnp.asarray(np.where(ok, 0.0, NEG), F32)


def _natten(q, kt, v, kct, vc, table):
    bsz, rows, w, d = q.shape
    l = vc.shape[1]
    qr_n = NA_KH
    kh = min(NA_KH, rows)
    band = qr_n // 2 + kh
    assert w == GRID_W and 2 * w == LANES and rows % (2 * qr_n) == 0
    assert (qr_n // 2 * w) % MXU_DIM == 0 and l % LANES == 0
    nhp = d // LANES
    image = pl.BlockSpec((1, rows, w, LANES), lambda hp, b: (b, 0, 0, hp))
    return pl.pallas_call(
        functools.partial(_natten_kernel, qr_n=qr_n, kh=kh),
        grid=(nhp, bsz),
        in_specs=[image,
                  pl.BlockSpec((LANES, rows * w), lambda hp, b: (hp, b)),
                  image,
                  pl.BlockSpec((LANES, l), lambda hp, b: (hp, b)),
                  pl.BlockSpec((1, l, LANES), lambda hp, b: (b, 0, hp)),
                  pl.BlockSpec((2,) + table.shape[1:], lambda hp, b: (hp, 0, 0, 0))],
        out_specs=image,
        out_shape=jax.ShapeDtypeStruct(q.shape, BF16),
        scratch_shapes=[pltpu.VMEM((3, 2, qr_n * w, band * w), F32)],
        compiler_params=_cparams("parallel", "arbitrary"),
        name="natten",
    )(q, kt, v, kct, vc, table)


def _tail_kernel(*refs, n_lhs, f_chunk):
    lhs = refs[:n_lhs]
    (wo_ref, x_ref, gpm_ref, gt1_ref, gpf_ref, sh2_ref, sc2_ref, wg_ref, wu_ref, wd_ref, gpo_ref,
     gt2_ref, o_ref) = refs[n_lhs:]
    hm = x_ref.shape[0] // 2

    def outproj(rs):
        y, k0 = None, 0
        for a_ref in lhs:
            k1 = k0 + a_ref.shape[1]
            part = jnp.dot(a_ref[rs, :], wo_ref[k0:k1, :], preferred_element_type=F32)
            y = part if y is None else y + part
            k0 = k1
        return y

    def norms(rs, y):
        x1 = x_ref[rs, :] + gt1_ref[0] * _rms(y, gpm_ref[...])
        fx = (_rms(x1, gpf_ref[...]) * (1.0 + sc2_ref[0]) + sh2_ref[0]).astype(BF16)
        return x1, fx

    def ffn(fx):
        acc = None
        for lo in range(0, wg_ref.shape[1], f_chunk):
            gate = jnp.dot(fx, wg_ref[:, lo:lo + f_chunk], preferred_element_type=F32)
            up = jnp.dot(fx, wu_ref[:, lo:lo + f_chunk], preferred_element_type=F32)
            hid = (_silu(gate) * up).astype(BF16)
            part = jnp.dot(hid, wd_ref[lo:lo + f_chunk, :], preferred_element_type=F32)
            acc = part if acc is None else acc + part
        return acc

    ra, rb = slice(0, hm), slice(hm, 2 * hm)
    ya = outproj(ra)
    yb = outproj(rb)
    x1a, fxa = norms(ra, ya)
    fa = ffn(fxa)
    x1b, fxb = norms(rb, yb)
    o_ref[ra, :] = x1a + gt2_ref[0] * _rms(fa, gpo_ref[...])
    fb = ffn(fxb)
    o_ref[rb, :] = x1b + gt2_ref[0] * _rms(fb, gpo_ref[...])


def _tail(lhs, wo, x2, gpm, gt1, gpf, sh2, sc2, layer, wg, wu, wd, gpo, gt2, rows_per_cond, cond_base):
    n, d = x2.shape
    f = wg.shape[2]
    tm = min(TAIL_TILE, n)
    assert f % TAIL_F_CHUNK == 0
    tiles_per_cond = rows_per_cond // tm
    cond_map = lambda i: (cond_base + i // tiles_per_cond, 0, 0)
    cond = pl.BlockSpec((1, 1, d), cond_map)
    vec = _const_spec((1, d))

    def layer_spec(w):
        return pl.BlockSpec((None,) + w.shape[1:], lambda i: (layer, 0, 0), pipeline_mode=pl.Buffered(1))
    return pl.pallas_call(
        functools.partial(_tail_kernel, n_lhs=len(lhs), f_chunk=TAIL_F_CHUNK),
        grid=(n // tm,),
        in_specs=([pl.BlockSpec((tm, a.shape[1]), lambda i: (i, 0)) for a in lhs]
                  + [_const_spec(wo.shape),
                     pl.BlockSpec((tm, d), lambda i: (i, 0)), vec, cond, vec, cond, cond,
                     layer_spec(wg), layer_spec(wu), layer_spec(wd), vec, cond]),
        out_specs=pl.BlockSpec((tm, d), lambda i: (i, 0)),
        out_shape=jax.ShapeDtypeStruct((n, d), F32),
        compiler_params=_cparams("parallel"),
        name="tail",
    )(*lhs, wo, x2, gpm.reshape(1, d), gt1, gpf.reshape(1, d), sh2, sc2, wg, wu, wd,
      gpo.reshape(1, d), gt2)


def kernel(x, c, ctx, c_ctx, w_mod, b_mod, g_pre_mix, g_post_mix, g_pre_ffn, g_post_ffn, w_ffn_gate, w_ffn_up, w_ffn_down, w_in_ab, conv_w, conv_b, lru_w_a, lru_b_a, lru_w_i, lru_b_i, lru_lam, w_out_ab, w_qkv_na, rpb_na, w_out_na):
    bsz, seq, d = x.shape
    ctx_len = ctx.shape[1]
    depth = w_mod.shape[0]
    lru_w = conv_w.shape[-1]
    n_cond = 8
    assert bsz < n_cond
    ctx_cond = bsz

    cond = jnp.concatenate([c, c_ctx[None], jnp.zeros((n_cond - bsz - 1, d), F32)], 0)
    mod = _mod_vectors(cond, w_mod, b_mod).reshape(depth, n_cond, 6, 1, d)

    xs = x.reshape(bsz * seq, d)
    cs = ctx.reshape(bsz * ctx_len, d)
    wg, wu, wd = (w.astype(BF16) for w in (w_ffn_gate, w_ffn_up, w_ffn_down))
    for l in range(depth):
        last = l == depth - 1
        sh1, sc1, gt1, sh2, sc2, gt2 = (mod[l, :, j] for j in range(6))
        norm_args = (g_post_mix[l], gt1, g_pre_ffn[l], sh2, sc2, l, wg, wu, wd, g_post_ffn[l], gt2)
        if l % 2 == 0:
            e = l // 2
            w_in = w_in_ab[e].astype(BF16)
            w_out = w_out_ab[e].astype(BF16)
            splits = ((0, lru_w, 1.0), (lru_w, 2 * lru_w, 1.0), (2 * lru_w, w_in.shape[1], 1.0))
            gates = [(_block_diag_halves(0.5 * lru_w_a[e, dr]), _block_diag_halves(0.5 * lru_w_i[e, dr]))
                     for dr in range(2)]

            def mixer(tokens, t_len, rows_per_cond, cond_base, h0):
                u, g, f = _inproj(tokens, g_pre_mix[l], sh1, sc1, w_in, splits, (F32, F32, F32),
                                  rows_per_cond, cond_base, slabbed=(2,))
                u, g = (a.reshape(bsz, t_len, -1) for a in (u, g))
                f = f.reshape(f.shape[0], bsz, t_len, LANES)
                lru = functools.partial(_lru_direction, u, cw=conv_w[e], cb=conv_b[e])
                hf, end_f = lru(h0[0], wa_bd=gates[0][0], wi_bd=gates[0][1], ba=0.5 * lru_b_a[e, 0],
                                bi=0.5 * lru_b_i[e, 0], lam=lru_lam[e, 0], reverse=False)
                lr, end_b = lru(h0[1], wa_bd=gates[1][0], wi_bd=gates[1][1], ba=0.5 * lru_b_a[e, 1],
                                bi=0.5 * lru_b_i[e, 1], lam=lru_lam[e, 1], reverse=True, other=hf, gate=g)
                fo = _fourier(f, f.shape[0] * LANES // FFT_GROUPS)
                lhs = [lr.reshape(tokens.shape[0], -1), fo.reshape(tokens.shape[0], -1)]
                return lhs, (end_f, end_b)

            zeros = jnp.zeros((bsz, 8, lru_w), F32)
            lhs_c, ends = mixer(cs, ctx_len, bsz * ctx_len, ctx_cond, (zeros, zeros))
            lhs_x, _ = mixer(xs, seq, seq, 0, ends)
            if not last:
                cs = _tail(lhs_c, w_out, cs, *norm_args, bsz * ctx_len, ctx_cond)
            xs = _tail(lhs_x, w_out, xs, *norm_args, seq, 0)
        else:
            o = l // 2
            w_qkv = w_qkv_na[o].astype(BF16)
            n_heads = rpb_na.shape[1]
            scale = LOG2E * (d // n_heads) ** -0.5
            q_split, v_split = (0, d, scale), (2 * d, 3 * d, 1.0)
            w_k_t = _transposed_block(w_qkv_na, o, 1, d)
            q, v, kt = _inproj(xs, g_pre_mix[l], sh1, sc1, w_qkv, (q_split, v_split), (BF16, BF16),
                               seq, 0, wt=w_k_t)
            vc, kct = _inproj(cs, g_pre_mix[l], sh1, sc1, w_qkv, (v_split,), (BF16,),
                              bsz * ctx_len, ctx_cond, wt=w_k_t)
            grid = (bsz, seq // GRID_W, GRID_W, d)
            att = _natten(q.reshape(grid), kt, v.reshape(grid), kct, vc.reshape(bsz, ctx_len, d),
                          _natten_bias_table(rpb_na[o]))
            if not last:
                raise NotImplementedError("context update after an attention layer")
            xs = _tail([att.reshape(bsz * seq, d)], w_out_na[o].astype(BF16), xs, *norm_args, seq, 0)
    return xs.reshape(bsz, seq, d)
```

```python
import functools
import math

import numpy as np
import jax
import jax.numpy as jnp
from jax import lax
from jax.experimental import pallas as pl
from jax.experimental.pallas import tpu as pltpu

F32 = jnp.float32
BF16 = jnp.bfloat16

EPS = 1e-6
LRU_C = 8.0
CONV_LEFT = 2
FFT_GROUPS = 8
GRID_W = 64
NA_KH = 8
NA_KW = 16
DFT_RADIX = 8
NEG = -1e30
LOG2E = math.log2(math.e)

VMEM_LIMIT_BYTES = 56 * 1024 * 1024
ROW_TILE = 1024
LRU_TILE = 512
LRU_BATCH = 2
TAIL_TILE = 1024
TAIL_F_CHUNK = 256
LANES = 128
MXU_DIM = 256


def _cparams(*sem):
    return pltpu.CompilerParams(dimension_semantics=sem, vmem_limit_bytes=VMEM_LIMIT_BYTES)


def _const_spec(shape):
    nd = len(shape)
    return pl.BlockSpec(shape, lambda *_: (0,) * nd, pipeline_mode=pl.Buffered(1))


def _sigmoid(x):
    return 0.5 * (1.0 + jnp.tanh(0.5 * x))


def _silu(x):
    return x * _sigmoid(x)


def _gelu_tanh(x):
    c = math.sqrt(2.0 / math.pi)
    return 0.5 * x * (1.0 + jnp.tanh(c * (x + 0.044715 * (x * x * x))))


def _rms(x, g):
    return x * lax.rsqrt(jnp.mean(x * x, axis=-1, keepdims=True) + EPS) * g


def _mod_kernel(cond_ref, w_ref, b_ref, o_ref):
    act = _silu(cond_ref[...]).astype(BF16)
    o_ref[0] = jnp.dot(act, w_ref[0].astype(BF16), preferred_element_type=F32) + b_ref[0]


def _mod_vectors(cond, w_mod, b_mod):
    depth, d, n = w_mod.shape
    tn = n // 4
    return pl.pallas_call(
        _mod_kernel,
        grid=(depth, n // tn),
        in_specs=[pl.BlockSpec(cond.shape, lambda l, j: (0, 0)),
                  pl.BlockSpec((1, d, tn), lambda l, j: (l, 0, j)),
                  pl.BlockSpec((1, 1, tn), lambda l, j: (l, 0, j))],
        out_specs=pl.BlockSpec((1, cond.shape[0], tn), lambda l, j: (l, 0, j)),
        out_shape=jax.ShapeDtypeStruct((depth, cond.shape[0], n), F32),
        compiler_params=_cparams("parallel", "parallel"),
        name="mod",
    )(cond, w_mod, b_mod.reshape(depth, 1, n))


def _transpose_kernel(x_ref, o_ref):
    o_ref[...] = x_ref[...].T.astype(o_ref.dtype)


def _transposed_block(w, layer, col_block, width):
    k = w.shape[1]
    return pl.pallas_call(
        _transpose_kernel,
        grid=(1,),
        in_specs=[pl.BlockSpec((None, k, width), lambda i: (layer, 0, col_block))],
        out_specs=pl.BlockSpec((width, k), lambda i: (0, 0)),
        out_shape=jax.ShapeDtypeStruct((width, k), BF16),
        compiler_params=_cparams("arbitrary"),
        name="transpose",
    )(w)


def _inproj_kernel(x_ref, g_ref, sh_ref, sc_ref, w_ref, *rest, splits, transposed):
    o_refs = rest
    if transposed:
        wt_ref, o_refs, ot_ref = rest[0], rest[1:-1], rest[-1]
    hm = x_ref.shape[0] // 2

    def norm(rs):
        h = _rms(x_ref[rs, :], g_ref[...]) * (1.0 + sc_ref[0]) + sh_ref[0]
        return h.astype(BF16)

    def project(rs, hb):
        if transposed:
            ot_ref[:, rs] = lax.dot_general(wt_ref[...], hb, (((1,), (1,)), ((), ())),
                                            preferred_element_type=F32).astype(ot_ref.dtype)
        for o_ref, (lo, hi, scale) in zip(o_refs, splits):
            z = jnp.dot(hb, w_ref[:, lo:hi], preferred_element_type=F32)
            if scale != 1.0:
                z = z * scale
            if len(o_ref.shape) == 3:
                for j in range(o_ref.shape[0]):
                    o_ref[j, rs, :] = z[:, j * LANES:(j + 1) * LANES].astype(o_ref.dtype)
            else:
                o_ref[rs, :] = z.astype(o_ref.dtype)

    ra, rb = slice(0, hm), slice(hm, 2 * hm)
    ha = norm(ra)
    hb = norm(rb)
    project(ra, ha)
    project(rb, hb)


def _inproj(x2, g, sh, sc, w, splits, dtypes, rows_per_cond, cond_base, slabbed=(), wt=None):
    n, d = x2.shape
    tm = min(ROW_TILE, n)
    tiles_per_cond = rows_per_cond // tm
    cond_map = lambda i: (cond_base + i // tiles_per_cond, 0, 0)
    out_specs, out_shape = [], []
    for j, ((lo, hi, _), dt) in enumerate(zip(splits, dtypes)):
        if j in slabbed:
            ns = (hi - lo) // LANES
            out_specs.append(pl.BlockSpec((ns, tm, LANES), lambda i: (0, i, 0)))
            out_shape.append(jax.ShapeDtypeStruct((ns, n, LANES), dt))
        else:
            out_specs.append(pl.BlockSpec((tm, hi - lo), lambda i: (i, 0)))
            out_shape.append(jax.ShapeDtypeStruct((n, hi - lo), dt))
    in_specs = [pl.BlockSpec((tm, d), lambda i: (i, 0)),
                _const_spec((1, d)),
                pl.BlockSpec((1, 1, d), cond_map),
                pl.BlockSpec((1, 1, d), cond_map),
                _const_spec(w.shape)]
    args = [x2, g.reshape(1, d), sh, sc, w]
    if wt is not None:
        in_specs.append(_const_spec(wt.shape))
        args.append(wt)
        out_specs.append(pl.BlockSpec((wt.shape[0], tm), lambda i: (0, i)))
        out_shape.append(jax.ShapeDtypeStruct((wt.shape[0], n), BF16))
    return pl.pallas_call(
        functools.partial(_inproj_kernel, splits=splits, transposed=wt is not None),
        grid=(n // tm,),
        in_specs=in_specs,
        out_specs=out_specs,
        out_shape=out_shape,
        compiler_params=_cparams("parallel"),
        name="inproj",
    )(*args)


def _lru_kernel(*refs, reverse, combine, tt, nt):
    if combine:
        (up_ref, u_ref, un_ref, h0_ref, cw_ref, cb_ref, wa_ref, wi_ref, ba_ref, bi_ref, lam_ref,
         ro_ref, g_ref, out_ref, hl_ref, ext_s, a_s, b_s, h_s) = refs
    else:
        (up_ref, u_ref, un_ref, h0_ref, cw_ref, cb_ref, wa_ref, wi_ref, ba_ref, bi_ref, lam_ref,
         out_ref, hl_ref, ext_s, a_s, b_s, h_s) = refs
    step = pl.program_id(1)
    tile = nt - 1 - step if reverse else step
    nb, _, c = u_ref.shape
    half = c // 2

    @pl.when(step == 0)
    def _():
        for bi in range(nb):
            h_s[bi] = h0_ref[bi, 0:1, :]

    lam = lam_ref[...]
    softplus_neg_lam = jnp.maximum(-lam, 0.0) + jnp.log1p(jnp.exp(-jnp.abs(lam)))
    coef = (-0.5 * LRU_C) * softplus_neg_lam
    for bi in range(nb):
        ext_s[bi, 0:8, :] = jnp.where(tile > 0, up_ref[bi], 0.0)
        ext_s[bi, 8:8 + tt, :] = u_ref[bi]
        ext_s[bi, 8 + tt:16 + tt, :] = jnp.where(tile < nt - 1, un_ref[bi], 0.0)
        ext = ext_s[bi]
        uc = cb_ref[...]
        for k in range(cw_ref.shape[0]):
            back = (CONV_LEFT - k) % (tt + 16)
            tap = ext if back == 0 else pltpu.roll(ext, back, 0)
            uc = uc + tap[8:8 + tt, :] * cw_ref[k:k + 1, :]

        ub = uc.astype(BF16)
        for hb in range(2):
            sl = slice(hb * half, (hb + 1) * half)
            tr = jnp.tanh(jnp.dot(ub[:, sl], wa_ref[hb], preferred_element_type=F32) + ba_ref[:, sl])
            ti = jnp.tanh(jnp.dot(ub[:, sl], wi_ref[hb], preferred_element_type=F32) + bi_ref[:, sl])
            log_a = coef[:, sl] * tr + coef[:, sl]
            a = jnp.exp(log_a)
            a_s[bi, :, sl] = a
            uh = 0.5 * uc[:, sl]
            w = -jnp.tanh(log_a) * (a * a + 1.0)
            root = jnp.where(w > 0.0, w * lax.rsqrt(w), 0.0)
            b_s[bi, :, sl] = root * (uh * ti + uh)

    row = lax.broadcasted_iota(jnp.int32, (8, c), 0)
    ng = tt // 8

    def group(jj, hs):
        j = ng - 1 - jj if reverse else jj
        off = pl.multiple_of(j * 8, 8)
        out = []
        for bi in range(nb):
            a = a_s[bi, pl.ds(off, 8), :]
            b = b_s[bi, pl.ds(off, 8), :]
            for k in (1, 2):
                shift = 8 - k if reverse else k
                keep = (row < 8 - k) if reverse else (row >= k)
                a_sh = pltpu.roll(a, shift, 0)
                b_sh = pltpu.roll(b, shift, 0)
                b = jnp.where(keep, a * b_sh + b, b)
                a = jnp.where(keep, a * a_sh, a)
            near = b + a * hs[bi]
            far = a * pltpu.roll(near, 4, 0) + b
            hj = jnp.where((row >= 4) if reverse else (row < 4), near, far)
            b_s[bi, pl.ds(off, 8), :] = hj
            out.append(hj[0:1, :] if reverse else hj[7:8, :])
        return tuple(out)

    h_last = lax.fori_loop(0, ng, group, tuple(h_s[bi] for bi in range(nb)), unroll=8)
    for bi in range(nb):
        h_s[bi] = h_last[bi]
        hl_ref[bi] = jnp.broadcast_to(h_last[bi], hl_ref.shape[1:])
        if combine:
            out_ref[bi] = ((ro_ref[bi] + b_s[bi]) * _gelu_tanh(g_ref[bi])).astype(out_ref.dtype)
        else:
            out_ref[bi] = b_s[bi]


def _lru_direction(u, h0, cw, cb, wa_bd, wi_bd, ba, bi, lam, reverse, other=None, gate=None):
    bsz, t, c = u.shape
    tt = min(LRU_TILE, t)
    nt = t // tt
    g8 = tt // 8
    nb = LRU_BATCH if bsz % LRU_BATCH == 0 else 1
    combine = other is not None
    tile = (lambda s: nt - 1 - s) if reverse else (lambda s: s)
    cur = pl.BlockSpec((nb, tt, c), lambda b, s: (b, tile(s), 0))
    state = pl.BlockSpec((nb, 8, c), lambda b, s: (b, 0, 0))
    in_specs = [pl.BlockSpec((nb, 8, c), lambda b, s: (b, jnp.maximum(tile(s) * g8 - 1, 0), 0)),
                cur,
                pl.BlockSpec((nb, 8, c), lambda b, s: (b, jnp.minimum((tile(s) + 1) * g8, t // 8 - 1), 0)),
                state,
                _const_spec(cw.shape), _const_spec((1, c)),
                _const_spec(wa_bd.shape), _const_spec(wi_bd.shape),
                _const_spec((1, c)), _const_spec((1, c)), _const_spec((1, c))]
    args = [u, u, u, h0, cw, cb.reshape(1, c), wa_bd, wi_bd, ba.reshape(1, c), bi.reshape(1, c),
            lam.reshape(1, c)]
    if combine:
        in_specs += [cur, cur]
        args += [other, gate]
    return pl.pallas_call(
        functools.partial(_lru_kernel, reverse=reverse, combine=combine, tt=tt, nt=nt),
        grid=(bsz // nb, nt),
        in_specs=in_specs,
        out_specs=[cur, state],
        out_shape=[jax.ShapeDtypeStruct((bsz, t, c), BF16 if combine else F32),
                   jax.ShapeDtypeStruct((bsz, 8, c), F32)],
        scratch_shapes=[pltpu.VMEM((nb, tt + 16, c), F32), pltpu.VMEM((nb, tt, c), F32),
                        pltpu.VMEM((nb, tt, c), F32), pltpu.VMEM((nb, 1, c), F32)],
        compiler_params=_cparams("parallel", "arbitrary"),
        name="lru_bwd" if reverse else "lru_fwd",
    )(*args)


def _block_diag_halves(w):
    h, d, _ = w.shape
    per = h // 2
    eye = jnp.eye(per, dtype=w.dtype)
    wh = w.reshape(2, per, d, d)
    bd = jnp.einsum("pq,cpij->cpiqj", eye, wh).reshape(2, per * d, per * d)
    return bd.astype(BF16)


def _fourier_kernel(x_ref, cs_ref, tc_ref, ts_ref, cc_ref, sc_ref, out_ref, y_s, *, n2):
    radix, _, cw = y_s.shape
    for n1 in range(radix):
        x = jnp.concatenate([x_ref[j, 0, pl.ds(n1, n2, stride=radix), :] for j in range(x_ref.shape[0])],
                            axis=-1)
        y_s[n1] = jnp.dot(cs_ref[...], x.astype(BF16), preferred_element_type=F32)

    def group(g, carry):
        r = pl.multiple_of(g * 8, 8)
        halves = [slice(h * LANES, (h + 1) * LANES) for h in range(cw // LANES)]
        loaded = [[(y_s[n1, pl.ds(r, 8), ls], y_s[n1, pl.ds(n2 + r, 8), ls]) for n1 in range(radix)]
                  for ls in halves]
        for ls, ys_half in zip(halves, loaded):
            zr, zi = [], []
            for n1, (yc, ys) in enumerate(ys_half):
                if n1 == 0:
                    zr.append(yc)
                    zi.append(-ys)
                else:
                    tc = tc_ref[n1 - 1, pl.ds(r, 8), :]
                    ts = ts_ref[n1 - 1, pl.ds(r, 8), :]
                    zr.append(yc * tc - ys * ts)
                    zi.append(-(yc * ts + ys * tc))
            ar, ai = _dft8(zr, zi)
            for k1 in range(radix):
                y_s[k1, pl.ds(r, 8), ls] = ar[k1]
                y_s[k1, pl.ds(n2 + r, 8), ls] = ai[k1]
        return carry

    lax.fori_loop(0, n2 // 8, group, 0, unroll=2)

    for k1 in range(radix):
        o = (jnp.dot(y_s[k1, 0:n2, :].astype(BF16), cc_ref[...], preferred_element_type=F32)
             + jnp.dot(y_s[k1, n2:2 * n2, :].astype(BF16), sc_ref[...], preferred_element_type=F32))
        out_ref[0, k1 * n2:(k1 + 1) * n2, :] = o.astype(out_ref.dtype)


def _dft8(zr, zi):
    h = math.sqrt(0.5)

    def dft4(r, i):
        s0r, s0i = r[0] + r[2], i[0] + i[2]
        s1r, s1i = r[0] - r[2], i[0] - i[2]
        s2r, s2i = r[1] + r[3], i[1] + i[3]
        s3r, s3i = r[1] - r[3], i[1] - i[3]
        return ([s0r + s2r, s1r + s3i, s0r - s2r, s1r - s3i],
                [s0i + s2i, s1i - s3r, s0i - s2i, s1i + s3r])

    er, ei = dft4(zr[0::2], zi[0::2])
    odr, odi = dft4(zr[1::2], zi[1::2])
    tr = [odr[0], h * (odr[1] + odi[1]), odi[2], h * (odi[3] - odr[3])]
    ti = [odi[0], h * (odi[1] - odr[1]), -odr[2], -h * (odr[3] + odi[3])]
    return ([er[k] + tr[k] for k in range(4)] + [er[k] - tr[k] for k in range(4)],
            [ei[k] + ti[k] for k in range(4)] + [ei[k] - ti[k] for k in range(4)])


def _fourier(f, group_dim):
    ns, bsz, t, _ = f.shape
    c = ns * LANES
    radix = DFT_RADIX
    n2 = t // radix
    cw = MXU_DIM
    nh = c // cw
    spb = cw // LANES
    k = np.arange(n2)
    ang = 2.0 * np.pi * ((k[:, None] * k[None, :]) % n2) / n2
    cs = np.concatenate([np.cos(ang), np.sin(ang)], 0) / math.sqrt(t)
    n1 = np.arange(1, radix)
    ang_t = 2.0 * np.pi * ((n1[:, None] * k[None, :]) % t) / t
    tw_shape = (radix - 1, n2, LANES)
    j = np.arange(cw)
    same = (j[:, None] // group_dim) == (j[None, :] // group_dim)
    ang_c = 2.0 * np.pi * ((j[:, None] * j[None, :]) % group_dim) / group_dim
    ccm = np.where(same, np.cos(ang_c), 0.0) / math.sqrt(group_dim)
    scm = np.where(same, np.sin(ang_c), 0.0) / math.sqrt(group_dim)
    assert radix == 8, "_dft8 is the radix stage"
    return pl.pallas_call(
        functools.partial(_fourier_kernel, n2=n2),
        grid=(bsz, nh),
        in_specs=[pl.BlockSpec((spb, 1, t, LANES), lambda b, h: (h, b, 0, 0)),
                  _const_spec((2 * n2, n2)),
                  _const_spec(tw_shape), _const_spec(tw_shape),
                  _const_spec((cw, cw)), _const_spec((cw, cw))],
        out_specs=pl.BlockSpec((1, t, cw), lambda b, h: (b, 0, h)),
        out_shape=jax.ShapeDtypeStruct((bsz, t, c), BF16),
        scratch_shapes=[pltpu.VMEM((radix, 2 * n2, cw), F32)],
        compiler_params=_cparams("parallel", "parallel"),
        name="fourier",
    )(f,
      jnp.asarray(cs, F32).astype(BF16),
      jnp.broadcast_to(jnp.asarray(np.cos(ang_t), F32)[:, :, None], tw_shape),
      jnp.broadcast_to(jnp.asarray(np.sin(ang_t), F32)[:, :, None], tw_shape),
      jnp.asarray(ccm, F32).astype(BF16), jnp.asarray(scm, F32).astype(BF16))


def _natten_kernel(q_ref, kt_ref, v_ref, kct_ref, vc_ref, tz_ref, o_ref, bias_s, *, qr_n, kh):
    rows, w, lanes = q_ref.shape[1:]
    nrb = rows // qr_n
    hd = lanes // 2
    half_n = qr_n // 2
    band = half_n + kh
    nq = half_n * w
    lane = lax.broadcasted_iota(jnp.int32, (1, lanes), 1)
    first = lane < hd
    ones = jnp.ones((1, lanes), BF16)
    mask = [(lane // hd == hh).astype(F32).astype(BF16) for hh in range(2)]
    vc = vc_ref[0]
    vc1 = jnp.concatenate([vc, jnp.broadcast_to(ones, vc.shape)], axis=1)

    def band_start(r_first):
        return jnp.clip(r_first - kh // 2, 0, rows - band)

    geometry_blocks = (0, 1, nrb - 1)

    def geometry(rb):
        return jnp.where(rb == 0, 0, jnp.where(rb == nrb - 1, 2, 1))

    def build_bias(geom):
        rb = geometry_blocks[geom]
        for half in range(2):
            r_first = rb * qr_n + half * half_n
            kr0 = min(max(r_first - kh // 2, 0), rows - band)
            for qr in range(half_n):
                r = r_first + qr
                rs = min(max(r - kh // 2, 0), rows - kh)
                for m in range(band // 2):
                    kr = kr0 + 2 * m
                    in_window = [rs <= kr + j < rs + kh for j in range(2)]
                    pair = kr - r + NA_KH
                    for hh in range(2):
                        row0 = hh * nq + qr * w
                        if not any(in_window):
                            tile = jnp.full((w, lanes), NEG, F32)
                        elif all(in_window):
                            tile = tz_ref[hh, pair]
                        else:
                            keep = (lane < hd) if in_window[0] else (lane >= hd)
                            tile = jnp.where(keep, tz_ref[hh, pair], NEG)
                        bias_s[geom, half, row0:row0 + w, m * lanes:(m + 1) * lanes] = tile

    def scores(rb, half):
        r_first = rb * qr_n + half * half_n
        kr0 = band_start(r_first)
        q = q_ref[0, pl.ds(r_first, half_n)].reshape(nq, lanes)
        qq = jnp.concatenate([q * mask[0], q * mask[1]], axis=0)
        kt = kt_ref[:, pl.ds(pl.multiple_of(kr0 * w, MXU_DIM), band * w)]
        s_loc = jnp.dot(qq, kt, preferred_element_type=F32) + bias_s[geometry(rb), half]
        s_ctx = jnp.dot(qq, kct_ref[...], preferred_element_type=F32)
        return s_loc, s_ctx

    def softmax(s):
        s_loc, s_ctx = s
        mx = jnp.maximum(jnp.max(s_loc, axis=-1, keepdims=True), jnp.max(s_ctx, axis=-1, keepdims=True))
        return jnp.exp2((s_loc - mx).astype(BF16)), jnp.exp2((s_ctx - mx).astype(BF16))

    def attend(rb, half, p):
        r_first = rb * qr_n + half * half_n
        vb = v_ref[0, pl.ds(band_start(r_first), band)].reshape(band * w, lanes)
        vb1 = jnp.concatenate([vb, jnp.broadcast_to(ones, vb.shape)], axis=1)
        acc = (jnp.dot(p[0], vb1, preferred_element_type=F32)
               + jnp.dot(p[1], vc1, preferred_element_type=F32))
        o0 = acc[:nq, :lanes] / acc[:nq, lanes:]
        o1 = acc[nq:, :lanes] / acc[nq:, lanes:]
        o_ref[0, pl.ds(r_first, half_n)] = jnp.where(first, o0, o1).reshape(
            half_n, w, lanes).astype(o_ref.dtype)

    @pl.when(pl.program_id(1) == 0)
    def _():
        for geom in range(len(geometry_blocks)):
            build_bias(geom)

    def row_block(rb, carry):
        s_a = scores(rb, 0)
        s_b = scores(rb, 1)
        p_a = softmax(s_a)
        attend(rb, 0, p_a)
        p_b = softmax(s_b)
        attend(rb, 1, p_b)
        return carry

    lax.fori_loop(0, nrb, row_block, 0, unroll=2)


def _natten_bias_table(rpb):
    n_off = rpb.shape[1]
    n_dc = rpb.shape[2]
    w = GRID_W
    qc = np.arange(w)
    cs = np.clip(qc - NA_KW // 2, 0, w - NA_KW)
    kc = np.arange(w)
    inside = (kc[None, :] >= cs[:, None]) & (kc[None, :] < cs[:, None] + NA_KW)
    dc = kc[None, :] - qc[:, None] + NA_KW - 1
    pick = (np.arange(n_dc)[:, None, None] == dc[None]) & inside[None]
    pick2 = np.zeros((2 * n_dc, w, 2 * w), np.float32)
    pick2[:n_dc, :, :w] = pick
    pick2[n_dc:, :, w:] = pick
    ext = jnp.pad(LOG2E * rpb, ((0, 0), (1, 1), (0, 0)))
    rows2 = jnp.concatenate([ext[:, :-1], ext[:, 1:]], axis=-1)
    table = jnp.einsum("hed,dqj->heqj", rows2, jnp.asarray(pick2), precision=lax.Precision.HIGHEST)
    slab_ok = np.arange(n_off + 2) - 1
    slab_ok = (slab_ok >= 0) & (slab_ok < n_off)
    ok = np.concatenate([slab_ok[:-1, None, None] & inside[None], slab_ok[1:, None, None] & inside[None]],
                        axis=-1)
    return table + jnp.asarray(np.where(ok, 0.0, NEG), F32)


def _natten(q, kt, v, kct, vc, table):
    bsz, rows, w, d = q.shape
    l = vc.shape[1]
    qr_n = NA_KH
    kh = min(NA_KH, rows)
    band = qr_n // 2 + kh
    assert w == GRID_W and 2 * w == LANES and rows % (2 * qr_n) == 0
    assert (qr_n // 2 * w) % MXU_DIM == 0 and l % LANES == 0
    nhp = d // LANES
    image = pl.BlockSpec((1, rows, w, LANES), lambda hp, b: (b, 0, 0, hp))
    return pl.pallas_call(
        functools.partial(_natten_kernel, qr_n=qr_n, kh=kh),
        grid=(nhp, bsz),
        in_specs=[image,
                  pl.BlockSpec((LANES, rows * w), lambda hp, b: (hp, b)),
                  image,
                  pl.BlockSpec((LANES, l), lambda hp, b: (hp, b)),
                  pl.BlockSpec((1, l, LANES), lambda hp, b: (b, 0, hp)),
                  pl.BlockSpec((2,) + table.shape[1:], lambda hp, b: (hp, 0, 0, 0))],
        out_specs=image,
        out_shape=jax.ShapeDtypeStruct(q.shape, BF16),
        scratch_shapes=[pltpu.VMEM((3, 2, qr_n * w, band * w), F32)],
        compiler_params=_cparams("parallel", "arbitrary"),
        name="natten",
    )(q, kt, v, kct, vc, table)


def _tail_kernel(*refs, n_lhs, f_chunk):
    lhs = refs[:n_lhs]
    (wo_ref, x_ref, gpm_ref, gt1_ref, gpf_ref, sh2_ref, sc2_ref, wg_ref, wu_ref, wd_ref, gpo_ref,
     gt2_ref, o_ref) = refs[n_lhs:]
    hm = x_ref.shape[0] // 2

    def outproj(rs):
        y, k0 = None, 0
        for a_ref in lhs:
            k1 = k0 + a_ref.shape[1]
            part = jnp.dot(a_ref[rs, :], wo_ref[k0:k1, :], preferred_element_type=F32)
            y = part if y is None else y + part
            k0 = k1
        return y

    def norms(rs, y):
        x1 = x_ref[rs, :] + gt1_ref[0] * _rms(y, gpm_ref[...])
        fx = (_rms(x1, gpf_ref[...]) * (1.0 + sc2_ref[0]) + sh2_ref[0]).astype(BF16)
        return x1, fx

    def ffn(fx):
        acc = None
        for lo in range(0, wg_ref.shape[1], f_chunk):
            gate = jnp.dot(fx, wg_ref[:, lo:lo + f_chunk], preferred_element_type=F32)
            up = jnp.dot(fx, wu_ref[:, lo:lo + f_chunk], preferred_element_type=F32)
            hid = (_silu(gate) * up).astype(BF16)
            part = jnp.dot(hid, wd_ref[lo:lo + f_chunk, :], preferred_element_type=F32)
            acc = part if acc is None else acc + part
        return acc

    ra, rb = slice(0, hm), slice(hm, 2 * hm)
    ya = outproj(ra)
    yb = outproj(rb)
    x1a, fxa = norms(ra, ya)
    fa = ffn(fxa)
    x1b, fxb = norms(rb, yb)
    o_ref[ra, :] = x1a + gt2_ref[0] * _rms(fa, gpo_ref[...])
    fb = ffn(fxb)
    o_ref[rb, :] = x1b + gt2_ref[0] * _rms(fb, gpo_ref[...])


def _tail(lhs, wo, x2, gpm, gt1, gpf, sh2, sc2, layer, wg, wu, wd, gpo, gt2, rows_per_cond, cond_base):
    n, d = x2.shape
    f = wg.shape[2]
    tm = min(TAIL_TILE, n)
    assert f % TAIL_F_CHUNK == 0
    tiles_per_cond = rows_per_cond // tm
    cond_map = lambda i: (cond_base + i // tiles_per_cond, 0, 0)
    cond = pl.BlockSpec((1, 1, d), cond_map)
    vec = _const_spec((1, d))

    def layer_spec(w):
        return pl.BlockSpec((None,) + w.shape[1:], lambda i: (layer, 0, 0), pipeline_mode=pl.Buffered(1))
    return pl.pallas_call(
        functools.partial(_tail_kernel, n_lhs=len(lhs), f_chunk=TAIL_F_CHUNK),
        grid=(n // tm,),
        in_specs=([pl.BlockSpec((tm, a.shape[1]), lambda i: (i, 0)) for a in lhs]
                  + [_const_spec(wo.shape),
                     pl.BlockSpec((tm, d), lambda i: (i, 0)), vec, cond, vec, cond, cond,
                     layer_spec(wg), layer_spec(wu), layer_spec(wd), vec, cond]),
        out_specs=pl.BlockSpec((tm, d), lambda i: (i, 0)),
        out_shape=jax.ShapeDtypeStruct((n, d), F32),
        compiler_params=_cparams("parallel"),
        name="tail",
    )(*lhs, wo, x2, gpm.reshape(1, d), gt1, gpf.reshape(1, d), sh2, sc2, wg, wu, wd,
      gpo.reshape(1, d), gt2)


def kernel(x, c, ctx, c_ctx, w_mod, b_mod, g_pre_mix, g_post_mix, g_pre_ffn, g_post_ffn, w_ffn_gate, w_ffn_up, w_ffn_down, w_in_ab, conv_w, conv_b, lru_w_a, lru_b_a, lru_w_i, lru_b_i, lru_lam, w_out_ab, w_qkv_na, rpb_na, w_out_na):
    bsz, seq, d = x.shape
    ctx_len = ctx.shape[1]
    depth = w_mod.shape[0]
    lru_w = conv_w.shape[-1]
    n_cond = 8
    assert bsz < n_cond
    ctx_cond = bsz

    cond = jnp.concatenate([c, c_ctx[None], jnp.zeros((n_cond - bsz - 1, d), F32)], 0)
    mod = _mod_vectors(cond, w_mod, b_mod).reshape(depth, n_cond, 6, 1, d)

    xs = x.reshape(bsz * seq, d)
    cs = ctx.reshape(bsz * ctx_len, d)
    wg, wu, wd = (w.astype(BF16) for w in (w_ffn_gate, w_ffn_up, w_ffn_down))
    for l in range(depth):
        last = l == depth - 1
        sh1, sc1, gt1, sh2, sc2, gt2 = (mod[l, :, j] for j in range(6))
        norm_args = (g_post_mix[l], gt1, g_pre_ffn[l], sh2, sc2, l, wg, wu, wd, g_post_ffn[l], gt2)
        if l % 2 == 0:
            e = l // 2
            w_in = w_in_ab[e].astype(BF16)
            w_out = w_out_ab[e].astype(BF16)
            splits = ((0, lru_w, 1.0), (lru_w, 2 * lru_w, 1.0), (2 * lru_w, w_in.shape[1], 1.0))
            gates = [(_block_diag_halves(0.5 * lru_w_a[e, dr]), _block_diag_halves(0.5 * lru_w_i[e, dr]))
                     for dr in range(2)]

            def mixer(tokens, t_len, rows_per_cond, cond_base, h0):
                u, g, f = _inproj(tokens, g_pre_mix[l], sh1, sc1, w_in, splits, (F32, F32, F32),
                                  rows_per_cond, cond_base, slabbed=(2,))
                u, g = (a.reshape(bsz, t_len, -1) for a in (u, g))
                f = f.reshape(f.shape[0], bsz, t_len, LANES)
                lru = functools.partial(_lru_direction, u, cw=conv_w[e], cb=conv_b[e])
                hf, end_f = lru(h0[0], wa_bd=gates[0][0], wi_bd=gates[0][1], ba=0.5 * lru_b_a[e, 0],
                                bi=0.5 * lru_b_i[e, 0], lam=lru_lam[e, 0], reverse=False)
                lr, end_b = lru(h0[1], wa_bd=gates[1][0], wi_bd=gates[1][1], ba=0.5 * lru_b_a[e, 1],
                                bi=0.5 * lru_b_i[e, 1], lam=lru_lam[e, 1], reverse=True, other=hf, gate=g)
                fo = _fourier(f, f.shape[0] * LANES // FFT_GROUPS)
                lhs = [lr.reshape(tokens.shape[0], -1), fo.reshape(tokens.shape[0], -1)]
                return lhs, (end_f, end_b)

            zeros = jnp.zeros((bsz, 8, lru_w), F32)
            lhs_c, ends = mixer(cs, ctx_len, bsz * ctx_len, ctx_cond, (zeros, zeros))
            lhs_x, _ = mixer(xs, seq, seq, 0, ends)
            if not last:
                cs = _tail(lhs_c, w_out, cs, *norm_args, bsz * ctx_len, ctx_cond)
            xs = _tail(lhs_x, w_out, xs, *norm_args, seq, 0)
        else:
            o = l // 2
            w_qkv = w_qkv_na[o].astype(BF16)
            n_heads = rpb_na.shape[1]
            scale = LOG2E * (d // n_heads) ** -0.5
            q_split, v_split = (0, d, scale), (2 * d, 3 * d, 1.0)
            w_k_t = _transposed_block(w_qkv_na, o, 1, d)
            q, v, kt = _inproj(xs, g_pre_mix[l], sh1, sc1, w_qkv, (q_split, v_split), (BF16, BF16),
                               seq, 0, wt=w_k_t)
            vc, kct = _inproj(cs, g_pre_mix[l], sh1, sc1, w_qkv, (v_split,), (BF16,),
                              bsz * ctx_len, ctx_cond, wt=w_k_t)
            grid = (bsz, seq // GRID_W, GRID_W, d)
            att = _natten(q.reshape(grid), kt, v.reshape(grid), kct, vc.reshape(bsz, ctx_len, d),
                          _natten_bias_table(rpb_na[o]))
            if not last:
                raise NotImplementedError("context update after an attention layer")
            xs = _tail([att.reshape(bsz * seq, d)], w_out_na[o].astype(BF16), xs, *norm_args, seq, 0)
    return xs.reshape(bsz, seq, d)
```

```python
import functools
import math

import numpy as np
import jax
import jax.numpy as jnp
from jax import lax
from jax.experimental import pallas as pl
from jax.experimental.pallas import tpu as pltpu

F32 = jnp.float32
BF16 = jnp.bfloat16

EPS = 1e-6
LRU_C = 8.0
CONV_LEFT = 2
FFT_GROUPS = 8
GRID_W = 64
NA_KH = 8
NA_KW = 16
DFT_RADIX = 8
NEG = -1e30
LOG2E = math.log2(math.e)

VMEM_LIMIT_BYTES = 56 * 1024 * 1024
ROW_TILE = 1024
LRU_TILE = 512
LRU_BATCH = 2
TAIL_TILE = 1024
TAIL_F_CHUNK = 256
LANES = 128
MXU_DIM = 256


def _cparams(*sem):
    return pltpu.CompilerParams(dimension_semantics=sem, vmem_limit_bytes=VMEM_LIMIT_BYTES)


def _const_spec(shape):
    nd = len(shape)
    return pl.BlockSpec(shape, lambda *_: (0,) * nd, pipeline_mode=pl.Buffered(1))


def _sigmoid(x):
    return 0.5 * (1.0 + jnp.tanh(0.5 * x))


def _silu(x):
    return x * _sigmoid(x)


def _gelu_tanh(x):
    c = math.sqrt(2.0 / math.pi)
    return 0.5 * x * (1.0 + jnp.tanh(c * (x + 0.044715 * (x * x * x))))


def _rms(x, g):
    return x * lax.rsqrt(jnp.mean(x * x, axis=-1, keepdims=True) + EPS) * g


def _mod_kernel(cond_ref, w_ref, b_ref, o_ref):
    act = _silu(cond_ref[...]).astype(BF16)
    o_ref[0] = jnp.dot(act, w_ref[0].astype(BF16), preferred_element_type=F32) + b_ref[0]


def _mod_vectors(cond, w_mod, b_mod):
    depth, d, n = w_mod.shape
    tn = n // 4
    return pl.pallas_call(
        _mod_kernel,
        grid=(depth, n // tn),
        in_specs=[pl.BlockSpec(cond.shape, lambda l, j: (0, 0)),
                  pl.BlockSpec((1, d, tn), lambda l, j: (l, 0, j)),
                  pl.BlockSpec((1, 1, tn), lambda l, j: (l, 0, j))],
        out_specs=pl.BlockSpec((1, cond.shape[0], tn), lambda l, j: (l, 0, j)),
        out_shape=jax.ShapeDtypeStruct((depth, cond.shape[0], n), F32),
        compiler_params=_cparams("parallel", "parallel"),
        name="mod",
    )(cond, w_mod, b_mod.reshape(depth, 1, n))


def _transpose_kernel(x_ref, o_ref):
    o_ref[...] = x_ref[...].T.astype(o_ref.dtype)


def _transposed_block(w, layer, col_block, width):
    k = w.shape[1]
    return pl.pallas_call(
        _transpose_kernel,
        grid=(1,),
        in_specs=[pl.BlockSpec((None, k, width), lambda i: (layer, 0, col_block))],
        out_specs=pl.BlockSpec((width, k), lambda i: (0, 0)),
        out_shape=jax.ShapeDtypeStruct((width, k), BF16),
        compiler_params=_cparams("arbitrary"),
        name="transpose",
    )(w)


def _inproj_kernel(x_ref, g_ref, sh_ref, sc_ref, w_ref, *rest, splits, transposed):
    o_refs = rest
    if transposed:
        wt_ref, o_refs, ot_ref = rest[0], rest[1:-1], rest[-1]
    hm = x_ref.shape[0] // 2
    gain = g_ref[...] * (1.0 + sc_ref[0])

    def norm(rs):
        h = _rms(x_ref[rs, :], gain) + sh_ref[0]
        return h.astype(BF16)

    def project(rs, hb):
        if transposed:
            ot_ref[:, rs] = lax.dot_general(wt_ref[...], hb, (((1,), (1,)), ((), ())),
                                            preferred_element_type=F32).astype(ot_ref.dtype)
        for o_ref, (lo, hi, scale) in zip(o_refs, splits):
            z = jnp.dot(hb, w_ref[:, lo:hi], preferred_element_type=F32)
            if scale != 1.0:
                z = z * scale
            if len(o_ref.shape) == 3:
                for j in range(o_ref.shape[0]):
                    o_ref[j, rs, :] = z[:, j * LANES:(j + 1) * LANES].astype(o_ref.dtype)
            else:
                o_ref[rs, :] = z.astype(o_ref.dtype)

    ra, rb = slice(0, hm), slice(hm, 2 * hm)
    ha = norm(ra)
    hb = norm(rb)
    project(ra, ha)
    project(rb, hb)


def _inproj(x2, g, sh, sc, w, splits, dtypes, rows_per_cond, cond_base, slabbed=(), wt=None):
    n, d = x2.shape
    tm = min(ROW_TILE, n)
    tiles_per_cond = rows_per_cond // tm
    cond_map = lambda i: (cond_base + i // tiles_per_cond, 0, 0)
    out_specs, out_shape = [], []
    for j, ((lo, hi, _), dt) in enumerate(zip(splits, dtypes)):
        if j in slabbed:
            ns = (hi - lo) // LANES
            out_specs.append(pl.BlockSpec((ns, tm, LANES), lambda i: (0, i, 0)))
            out_shape.append(jax.ShapeDtypeStruct((ns, n, LANES), dt))
        else:
            out_specs.append(pl.BlockSpec((tm, hi - lo), lambda i: (i, 0)))
            out_shape.append(jax.ShapeDtypeStruct((n, hi - lo), dt))
    in_specs = [pl.BlockSpec((tm, d), lambda i: (i, 0)),
                _const_spec((1, d)),
                pl.BlockSpec((1, 1, d), cond_map),
                pl.BlockSpec((1, 1, d), cond_map),
                _const_spec(w.shape)]
    args = [x2, g.reshape(1, d), sh, sc, w]
    if wt is not None:
        in_specs.append(_const_spec(wt.shape))
        args.append(wt)
        out_specs.append(pl.BlockSpec((wt.shape[0], tm), lambda i: (0, i)))
        out_shape.append(jax.ShapeDtypeStruct((wt.shape[0], n), BF16))
    return pl.pallas_call(
        functools.partial(_inproj_kernel, splits=splits, transposed=wt is not None),
        grid=(n // tm,),
        in_specs=in_specs,
        out_specs=out_specs,
        out_shape=out_shape,
        compiler_params=_cparams("parallel"),
        name="inproj",
    )(*args)


def _lru_kernel(*refs, reverse, combine, tt, nt):
    if combine:
        (up_ref, u_ref, un_ref, h0_ref, cw_ref, cb_ref, wa_ref, wi_ref, ba_ref, bi_ref, lam_ref,
         ro_ref, g_ref, out_ref, hl_ref, ext_s, a_s, b_s, h_s) = refs
    else:
        (up_ref, u_ref, un_ref, h0_ref, cw_ref, cb_ref, wa_ref, wi_ref, ba_ref, bi_ref, lam_ref,
         out_ref, hl_ref, ext_s, a_s, b_s, h_s) = refs
    step = pl.program_id(1)
    tile = nt - 1 - step if reverse else step
    nb, _, c = u_ref.shape
    half = c // 2

    @pl.when(step == 0)
    def _():
        for bi in range(nb):
            h_s[bi] = h0_ref[bi, 0:1, :]

    lam = lam_ref[...]
    softplus_neg_lam = jnp.maximum(-lam, 0.0) + jnp.log1p(jnp.exp(-jnp.abs(lam)))
    coef = (-0.5 * LRU_C) * softplus_neg_lam
    for bi in range(nb):
        ext_s[bi, 0:8, :] = jnp.where(tile > 0, up_ref[bi], 0.0)
        ext_s[bi, 8:8 + tt, :] = u_ref[bi]
        ext_s[bi, 8 + tt:16 + tt, :] = jnp.where(tile < nt - 1, un_ref[bi], 0.0)
        ext = ext_s[bi]
        uc = cb_ref[...]
        for k in range(cw_ref.shape[0]):
            back = (CONV_LEFT - k) % (tt + 16)
            tap = ext if back == 0 else pltpu.roll(ext, back, 0)
            uc = uc + tap[8:8 + tt, :] * cw_ref[k:k + 1, :]

        ub = uc.astype(BF16)
        for hb in range(2):
            sl = slice(hb * half, (hb + 1) * half)
            tr = jnp.tanh(jnp.dot(ub[:, sl], wa_ref[hb], preferred_element_type=F32) + ba_ref[:, sl])
            ti = jnp.tanh(jnp.dot(ub[:, sl], wi_ref[hb], preferred_element_type=F32) + bi_ref[:, sl])
            log_a = coef[:, sl] * tr + coef[:, sl]
            a = jnp.exp(log_a)
            a_s[bi, :, sl] = a
            uh = 0.5 * uc[:, sl]
            w = -jnp.tanh(log_a) * (a * a + 1.0)
            root = jnp.where(w > 0.0, w * lax.rsqrt(w), 0.0)
            b_s[bi, :, sl] = root * (uh * ti + uh)

    row = lax.broadcasted_iota(jnp.int32, (8, c), 0)
    ng = tt // 8

    def group(jj, hs):
        j = ng - 1 - jj if reverse else jj
        off = pl.multiple_of(j * 8, 8)
        out = []
        for bi in range(nb):
            a = a_s[bi, pl.ds(off, 8), :]
            b = b_s[bi, pl.ds(off, 8), :]
            for k in (1, 2):
                shift = 8 - k if reverse else k
                keep = (row < 8 - k) if reverse else (row >= k)
                a_sh = pltpu.roll(a, shift, 0)
                b_sh = pltpu.roll(b, shift, 0)
                b = jnp.where(keep, a * b_sh + b, b)
                a = jnp.where(keep, a * a_sh, a)
            near = b + a * hs[bi]
            far = a * pltpu.roll(near, 4, 0) + b
            hj = jnp.where((row >= 4) if reverse else (row < 4), near, far)
            b_s[bi, pl.ds(off, 8), :] = hj
            out.append(hj[0:1, :] if reverse else hj[7:8, :])
        return tuple(out)

    h_last = lax.fori_loop(0, ng, group, tuple(h_s[bi] for bi in range(nb)), unroll=8)
    for bi in range(nb):
        h_s[bi] = h_last[bi]
        hl_ref[bi] = jnp.broadcast_to(h_last[bi], hl_ref.shape[1:])
        if combine:
            out_ref[bi] = ((ro_ref[bi] + b_s[bi]) * _gelu_tanh(g_ref[bi])).astype(out_ref.dtype)
        else:
            out_ref[bi] = b_s[bi]


def _lru_direction(u, h0, cw, cb, wa_bd, wi_bd, ba, bi, lam, reverse, other=None, gate=None):
    bsz, t, c = u.shape
    tt = min(LRU_TILE, t)
    nt = t // tt
    g8 = tt // 8
    nb = LRU_BATCH if bsz % LRU_BATCH == 0 else 1
    combine = other is not None
    tile = (lambda s: nt - 1 - s) if reverse else (lambda s: s)
    cur = pl.BlockSpec((nb, tt, c), lambda b, s: (b, tile(s), 0))
    state = pl.BlockSpec((nb, 8, c), lambda b, s: (b, 0, 0))
    in_specs = [pl.BlockSpec((nb, 8, c), lambda b, s: (b, jnp.maximum(tile(s) * g8 - 1, 0), 0)),
                cur,
                pl.BlockSpec((nb, 8, c), lambda b, s: (b, jnp.minimum((tile(s) + 1) * g8, t // 8 - 1), 0)),
                state,
                _const_spec(cw.shape), _const_spec((1, c)),
                _const_spec(wa_bd.shape), _const_spec(wi_bd.shape),
                _const_spec((1, c)), _const_spec((1, c)), _const_spec((1, c))]
    args = [u, u, u, h0, cw, cb.reshape(1, c), wa_bd, wi_bd, ba.reshape(1, c), bi.reshape(1, c),
            lam.reshape(1, c)]
    if combine:
        in_specs += [cur, cur]
        args += [other, gate]
    return pl.pallas_call(
        functools.partial(_lru_kernel, reverse=reverse, combine=combine, tt=tt, nt=nt),
        grid=(bsz // nb, nt),
        in_specs=in_specs,
        out_specs=[cur, state],
        out_shape=[jax.ShapeDtypeStruct((bsz, t, c), BF16 if combine else F32),
                   jax.ShapeDtypeStruct((bsz, 8, c), F32)],
        scratch_shapes=[pltpu.VMEM((nb, tt + 16, c), F32), pltpu.VMEM((nb, tt, c), F32),
                        pltpu.VMEM((nb, tt, c), F32), pltpu.VMEM((nb, 1, c), F32)],
        compiler_params=_cparams("parallel", "arbitrary"),
        name="lru_bwd" if reverse else "lru_fwd",
    )(*args)


def _block_diag_halves(w):
    h, d, _ = w.shape
    per = h // 2
    eye = jnp.eye(per, dtype=w.dtype)
    wh = w.reshape(2, per, d, d)
    bd = jnp.einsum("pq,cpij->cpiqj", eye, wh).reshape(2, per * d, per * d)
    return bd.astype(BF16)


def _fourier_kernel(x_ref, cs_ref, tc_ref, ts_ref, cc_ref, sc_ref, out_ref, y_s, *, n2):
    radix, _, cw = y_s.shape
    for n1 in range(radix):
        x = jnp.concatenate([x_ref[j, 0, pl.ds(n1, n2, stride=radix), :] for j in range(x_ref.shape[0])],
                            axis=-1)
        y_s[n1] = jnp.dot(cs_ref[...], x.astype(BF16), preferred_element_type=F32)

    def group(g, carry):
        r = pl.multiple_of(g * 8, 8)
        halves = [slice(h * LANES, (h + 1) * LANES) for h in range(cw // LANES)]
        loaded = [[(y_s[n1, pl.ds(r, 8), ls], y_s[n1, pl.ds(n2 + r, 8), ls]) for n1 in range(radix)]
                  for ls in halves]
        for ls, ys_half in zip(halves, loaded):
            zr, zi = [], []
            for n1, (yc, ys) in enumerate(ys_half):
                if n1 == 0:
                    zr.append(yc)
                    zi.append(-ys)
                else:
                    tc = tc_ref[n1 - 1, pl.ds(r, 8), :]
                    ts = ts_ref[n1 - 1, pl.ds(r, 8), :]
                    zr.append(yc * tc - ys * ts)
                    zi.append(-(yc * ts + ys * tc))
            ar, ai = _dft8(zr, zi)
            for k1 in range(radix):
                y_s[k1, pl.ds(r, 8), ls] = ar[k1]
                y_s[k1, pl.ds(n2 + r, 8), ls] = ai[k1]
        return carry

    lax.fori_loop(0, n2 // 8, group, 0, unroll=2)

    for k1 in range(radix):
        o = (jnp.dot(y_s[k1, 0:n2, :].astype(BF16), cc_ref[...], preferred_element_type=F32)
             + jnp.dot(y_s[k1, n2:2 * n2, :].astype(BF16), sc_ref[...], preferred_element_type=F32))
        out_ref[0, k1 * n2:(k1 + 1) * n2, :] = o.astype(out_ref.dtype)


def _dft8(zr, zi):
    h = math.sqrt(0.5)

    def dft4(r, i):
        s0r, s0i = r[0] + r[2], i[0] + i[2]
        s1r, s1i = r[0] - r[2], i[0] - i[2]
        s2r, s2i = r[1] + r[3], i[1] + i[3]
        s3r, s3i = r[1] - r[3], i[1] - i[3]
        return ([s0r + s2r, s1r + s3i, s0r - s2r, s1r - s3i],
                [s0i + s2i, s1i - s3r, s0i - s2i, s1i + s3r])

    er, ei = dft4(zr[0::2], zi[0::2])
    odr, odi = dft4(zr[1::2], zi[1::2])
    tr = [odr[0], h * (odr[1] + odi[1]), odi[2], h * (odi[3] - odr[3])]
    ti = [odi[0], h * (odi[1] - odr[1]), -odr[2], -h * (odr[3] + odi[3])]
    return ([er[k] + tr[k] for k in range(4)] + [er[k] - tr[k] for k in range(4)],
            [ei[k] + ti[k] for k in range(4)] + [ei[k] - ti[k] for k in range(4)])


def _fourier(f, group_dim):
    ns, bsz, t, _ = f.shape
    c = ns * LANES
    radix = DFT_RADIX
    n2 = t // radix
    cw = MXU_DIM
    nh = c // cw
    spb = cw // LANES
    k = np.arange(n2)
    ang = 2.0 * np.pi * ((k[:, None] * k[None, :]) % n2) / n2
    cs = np.concatenate([np.cos(ang), np.sin(ang)], 0) / math.sqrt(t)
    n1 = np.arange(1, radix)
    ang_t = 2.0 * np.pi * ((n1[:, None] * k[None, :]) % t) / t
    tw_shape = (radix - 1, n2, LANES)
    j = np.arange(cw)
    same = (j[:, None] // group_dim) == (j[None, :] // group_dim)
    ang_c = 2.0 * np.pi * ((j[:, None] * j[None, :]) % group_dim) / group_dim
    ccm = np.where(same, np.cos(ang_c), 0.0) / math.sqrt(group_dim)
    scm = np.where(same, np.sin(ang_c), 0.0) / math.sqrt(group_dim)
    assert radix == 8, "_dft8 is the radix stage"
    return pl.pallas_call(
        functools.partial(_fourier_kernel, n2=n2),
        grid=(bsz, nh),
        in_specs=[pl.BlockSpec((spb, 1, t, LANES), lambda b, h: (h, b, 0, 0)),
                  _const_spec((2 * n2, n2)),
                  _const_spec(tw_shape), _const_spec(tw_shape),
                  _const_spec((cw, cw)), _const_spec((cw, cw))],
        out_specs=pl.BlockSpec((1, t, cw), lambda b, h: (b, 0, h)),
        out_shape=jax.ShapeDtypeStruct((bsz, t, c), BF16),
        scratch_shapes=[pltpu.VMEM((radix, 2 * n2, cw), F32)],
        compiler_params=_cparams("parallel", "parallel"),
        name="fourier",
    )(f,
      jnp.asarray(cs, F32).astype(BF16),
      jnp.broadcast_to(jnp.asarray(np.cos(ang_t), F32)[:, :, None], tw_shape),
      jnp.broadcast_to(jnp.asarray(np.sin(ang_t), F32)[:, :, None], tw_shape),
      jnp.asarray(ccm, F32).astype(BF16), jnp.asarray(scm, F32).astype(BF16))


def _natten_kernel(q_ref, kt_ref, v_ref, kct_ref, vc_ref, tz_ref, o_ref, bias_s, *, qr_n, kh):
    rows, w, lanes = q_ref.shape[1:]
    nrb = rows // qr_n
    hd = lanes // 2
    half_n = qr_n // 2
    band = half_n + kh
    nq = half_n * w
    lane = lax.broadcasted_iota(jnp.int32, (1, lanes), 1)
    first = lane < hd
    ones = jnp.ones((1, lanes), BF16)
    mask = [(lane // hd == hh).astype(F32).astype(BF16) for hh in range(2)]
    vc = vc_ref[0]
    vc1 = jnp.concatenate([vc, jnp.broadcast_to(ones, vc.shape)], axis=1)

    def band_start(r_first):
        return jnp.clip(r_first - kh // 2, 0, rows - band)

    geometry_blocks = (0, 1, nrb - 1)

    def geometry(rb):
        return jnp.where(rb == 0, 0, jnp.where(rb == nrb - 1, 2, 1))

    def build_bias(geom):
        rb = geometry_blocks[geom]
        for half in range(2):
            r_first = rb * qr_n + half * half_n
            kr0 = min(max(r_first - kh // 2, 0), rows - band)
            for qr in range(half_n):
                r = r_first + qr
                rs = min(max(r - kh // 2, 0), rows - kh)
                for m in range(band // 2):
                    kr = kr0 + 2 * m
                    in_window = [rs <= kr + j < rs + kh for j in range(2)]
                    pair = kr - r + NA_KH
                    for hh in range(2):
                        row0 = hh * nq + qr * w
                        if not any(in_window):
                            tile = jnp.full((w, lanes), NEG, F32)
                        elif all(in_window):
                            tile = tz_ref[hh, pair]
                        else:
                            keep = (lane < hd) if in_window[0] else (lane >= hd)
                            tile = jnp.where(keep, tz_ref[hh, pair], NEG)
                        bias_s[geom, half, row0:row0 + w, m * lanes:(m + 1) * lanes] = tile

    def scores(rb, half):
        r_first = rb * qr_n + half * half_n
        kr0 = band_start(r_first)
        q = q_ref[0, pl.ds(r_first, half_n)].reshape(nq, lanes)
        qq = jnp.concatenate([q * mask[0], q * mask[1]], axis=0)
        kt = kt_ref[:, pl.ds(pl.multiple_of(kr0 * w, MXU_DIM), band * w)]
        s_loc = jnp.dot(qq, kt, preferred_element_type=F32) + bias_s[geometry(rb), half]
        s_ctx = jnp.dot(qq, kct_ref[...], preferred_element_type=F32)
        return s_loc, s_ctx

    def softmax(s):
        s_loc, s_ctx = s
        mx = jnp.maximum(jnp.max(s_loc, axis=-1, keepdims=True), jnp.max(s_ctx, axis=-1, keepdims=True))
        return jnp.exp2((s_loc - mx).astype(BF16)), jnp.exp2((s_ctx - mx).astype(BF16))

    def attend(rb, half, p):
        r_first = rb * qr_n + half * half_n
        vb = v_ref[0, pl.ds(band_start(r_first), band)].reshape(band * w, lanes)
        vb1 = jnp.concatenate([vb, jnp.broadcast_to(ones, vb.shape)], axis=1)
        acc = (jnp.dot(p[0], vb1, preferred_element_type=F32)
               + jnp.dot(p[1], vc1, preferred_element_type=F32))
        o0 = acc[:nq, :lanes] / acc[:nq, lanes:]
        o1 = acc[nq:, :lanes] / acc[nq:, lanes:]
        o_ref[0, pl.ds(r_first, half_n)] = jnp.where(first, o0, o1).reshape(
            half_n, w, lanes).astype(o_ref.dtype)

    @pl.when(pl.program_id(1) == 0)
    def _():
        for geom in range(len(geometry_blocks)):
            build_bias(geom)

    def row_block(rb, carry):
        s_a = scores(rb, 0)
        s_b = scores(rb, 1)
        p_a = softmax(s_a)
        attend(rb, 0, p_a)
        p_b = softmax(s_b)
        attend(rb, 1, p_b)
        return carry

    lax.fori_loop(0, nrb, row_block, 0, unroll=2)


def _natten_bias_table(rpb):
    n_off = rpb.shape[1]
    n_dc = rpb.shape[2]
    w = GRID_W
    qc = np.arange(w)
    cs = np.clip(qc - NA_KW // 2, 0, w - NA_KW)
    kc = np.arange(w)
    inside = (kc[None, :] >= cs[:, None]) & (kc[None, :] < cs[:, None] + NA_KW)
    dc = kc[None, :] - qc[:, None] + NA_KW - 1
    pick = (np.arange(n_dc)[:, None, None] == dc[None]) & inside[None]
    pick2 = np.zeros((2 * n_dc, w, 2 * w), np.float32)
    pick2[:n_dc, :, :w] = pick
    pick2[n_dc:, :, w:] = pick
    ext = jnp.pad(LOG2E * rpb, ((0, 0), (1, 1), (0, 0)))
    rows2 = jnp.concatenate([ext[:, :-1], ext[:, 1:]], axis=-1)
    table = jnp.einsum("hed,dqj->heqj", rows2, jnp.asarray(pick2), precision=lax.Precision.HIGHEST)
    slab_ok = np.arange(n_off + 2) - 1
    slab_ok = (slab_ok >= 0) & (slab_ok < n_off)
    ok = np.concatenate([slab_ok[:-1, None, None] & inside[None], slab_ok[1:, None, None] & inside[None]],
                        axis=-1)
    return table + jnp.asarray(np.where(ok, 0.0, NEG), F32)


def _natten(q, kt, v, kct, vc, table):
    bsz, rows, w, d = q.shape
    l = vc.shape[1]
    qr_n = NA_KH
    kh = min(NA_KH, rows)
    band = qr_n // 2 + kh
    assert w == GRID_W and 2 * w == LANES and rows % (2 * qr_n) == 0
    assert (qr_n // 2 * w) % MXU_DIM == 0 and l % LANES == 0
    nhp = d // LANES
    image = pl.BlockSpec((1, rows, w, LANES), lambda hp, b: (b, 0, 0, hp))
    return pl.pallas_call(
        functools.partial(_natten_kernel, qr_n=qr_n, kh=kh),
        grid=(nhp, bsz),
        in_specs=[image,
                  pl.BlockSpec((LANES, rows * w), lambda hp, b: (hp, b)),
                  image,
                  pl.BlockSpec((LANES, l), lambda hp, b: (hp, b)),
                  pl.BlockSpec((1, l, LANES), lambda hp, b: (b, 0, hp)),
                  pl.BlockSpec((2,) + table.shape[1:], lambda hp, b: (hp, 0, 0, 0))],
        out_specs=image,
        out_shape=jax.ShapeDtypeStruct(q.shape, BF16),
        scratch_shapes=[pltpu.VMEM((3, 2, qr_n * w, band * w), F32)],
        compiler_params=_cparams("parallel", "arbitrary"),
        name="natten",
    )(q, kt, v, kct, vc, table)


def _tail_kernel(*refs, n_lhs, f_chunk):
    lhs = refs[:n_lhs]
    (wo_ref, x_ref, gpm_ref, gt1_ref, gpf_ref, sh2_ref, sc2_ref, wg_ref, wu_ref, wd_ref, gpo_ref,
     gt2_ref, o_ref) = refs[n_lhs:]
    hm = x_ref.shape[0] // 2
    gain_mix = gt1_ref[0] * gpm_ref[...]
    gain_ffn = gpf_ref[...] * (1.0 + sc2_ref[0])
    gain_out = gt2_ref[0] * gpo_ref[...]

    def outproj(rs):
        y, k0 = None, 0
        for a_ref in lhs:
            k1 = k0 + a_ref.shape[1]
            part = jnp.dot(a_ref[rs, :], wo_ref[k0:k1, :], preferred_element_type=F32)
            y = part if y is None else y + part
            k0 = k1
        return y

    def norms(rs, y):
        x1 = x_ref[rs, :] + _rms(y, gain_mix)
        fx = (_rms(x1, gain_ffn) + sh2_ref[0]).astype(BF16)
        return x1, fx

    def ffn(fx):
        acc = None
        for lo in range(0, wg_ref.shape[1], f_chunk):
            gate = jnp.dot(fx, wg_ref[:, lo:lo + f_chunk], preferred_element_type=F32)
            up = jnp.dot(fx, wu_ref[:, lo:lo + f_chunk], preferred_element_type=F32)
            hid = (_silu(gate) * up).astype(BF16)
            part = jnp.dot(hid, wd_ref[lo:lo + f_chunk, :], preferred_element_type=F32)
            acc = part if acc is None else acc + part
        return acc

    ra, rb = slice(0, hm), slice(hm, 2 * hm)
    ya = outproj(ra)
    yb = outproj(rb)
    x1a, fxa = norms(ra, ya)
    fa = ffn(fxa)
    x1b, fxb = norms(rb, yb)
    o_ref[ra, :] = x1a + _rms(fa, gain_out)
    fb = ffn(fxb)
    o_ref[rb, :] = x1b + _rms(fb, gain_out)


def _tail(lhs, wo, x2, gpm, gt1, gpf, sh2, sc2, layer, wg, wu, wd, gpo, gt2, rows_per_cond, cond_base):
    n, d = x2.shape
    f = wg.shape[2]
    tm = min(TAIL_TILE, n)
    assert f % TAIL_F_CHUNK == 0
    tiles_per_cond = rows_per_cond // tm
    cond_map = lambda i: (cond_base + i // tiles_per_cond, 0, 0)
    cond = pl.BlockSpec((1, 1, d), cond_map)
    vec = _const_spec((1, d))

    def layer_spec(w):
        return pl.BlockSpec((None,) + w.shape[1:], lambda i: (layer, 0, 0), pipeline_mode=pl.Buffered(1))
    return pl.pallas_call(
        functools.partial(_tail_kernel, n_lhs=len(lhs), f_chunk=TAIL_F_CHUNK),
        grid=(n // tm,),
        in_specs=([pl.BlockSpec((tm, a.shape[1]), lambda i: (i, 0)) for a in lhs]
                  + [_const_spec(wo.shape),
                     pl.BlockSpec((tm, d), lambda i: (i, 0)), vec, cond, vec, cond, cond,
                     layer_spec(wg), layer_spec(wu), layer_spec(wd), vec, cond]),
        out_specs=pl.BlockSpec((tm, d), lambda i: (i, 0)),
        out_shape=jax.ShapeDtypeStruct((n, d), F32),
        compiler_params=_cparams("parallel"),
        name="tail",
    )(*lhs, wo, x2, gpm.reshape(1, d), gt1, gpf.reshape(1, d), sh2, sc2, wg, wu, wd,
      gpo.reshape(1, d), gt2)


def kernel(x, c, ctx, c_ctx, w_mod, b_mod, g_pre_mix, g_post_mix, g_pre_ffn, g_post_ffn, w_ffn_gate, w_ffn_up, w_ffn_down, w_in_ab, conv_w, conv_b, lru_w_a, lru_b_a, lru_w_i, lru_b_i, lru_lam, w_out_ab, w_qkv_na, rpb_na, w_out_na):
    bsz, seq, d = x.shape
    ctx_len = ctx.shape[1]
    depth = w_mod.shape[0]
    lru_w = conv_w.shape[-1]
    n_cond = 8
    assert bsz < n_cond
    ctx_cond = bsz

    cond = jnp.concatenate([c, c_ctx[None], jnp.zeros((n_cond - bsz - 1, d), F32)], 0)
    mod = _mod_vectors(cond, w_mod, b_mod).reshape(depth, n_cond, 6, 1, d)

    xs = x.reshape(bsz * seq, d)
    cs = ctx.reshape(bsz * ctx_len, d)
    wg, wu, wd = (w.astype(BF16) for w in (w_ffn_gate, w_ffn_up, w_ffn_down))
    for l in range(depth):
        last = l == depth - 1
        sh1, sc1, gt1, sh2, sc2, gt2 = (mod[l, :, j] for j in range(6))
        norm_args = (g_post_mix[l], gt1, g_pre_ffn[l], sh2, sc2, l, wg, wu, wd, g_post_ffn[l], gt2)
        if l % 2 == 0:
            e = l // 2
            w_in = w_in_ab[e].astype(BF16)
            w_out = w_out_ab[e].astype(BF16)
            splits = ((0, lru_w, 1.0), (lru_w, 2 * lru_w, 1.0), (2 * lru_w, w_in.shape[1], 1.0))
            gates = [(_block_diag_halves(0.5 * lru_w_a[e, dr]), _block_diag_halves(0.5 * lru_w_i[e, dr]))
                     for dr in range(2)]

            def mixer(tokens, t_len, rows_per_cond, cond_base, h0):
                u, g, f = _inproj(tokens, g_pre_mix[l], sh1, sc1, w_in, splits, (F32, F32, F32),
                                  rows_per_cond, cond_base, slabbed=(2,))
                u, g = (a.reshape(bsz, t_len, -1) for a in (u, g))
                f = f.reshape(f.shape[0], bsz, t_len, LANES)
                lru = functools.partial(_lru_direction, u, cw=conv_w[e], cb=conv_b[e])
                hf, end_f = lru(h0[0], wa_bd=gates[0][0], wi_bd=gates[0][1], ba=0.5 * lru_b_a[e, 0],
                                bi=0.5 * lru_b_i[e, 0], lam=lru_lam[e, 0], reverse=False)
                lr, end_b = lru(h0[1], wa_bd=gates[1][0], wi_bd=gates[1][1], ba=0.5 * lru_b_a[e, 1],
                                bi=0.5 * lru_b_i[e, 1], lam=lru_lam[e, 1], reverse=True, other=hf, gate=g)
                fo = _fourier(f, f.shape[0] * LANES // FFT_GROUPS)
                lhs = [lr.reshape(tokens.shape[0], -1), fo.reshape(tokens.shape[0], -1)]
                return lhs, (end_f, end_b)

            zeros = jnp.zeros((bsz, 8, lru_w), F32)
            lhs_c, ends = mixer(cs, ctx_len, bsz * ctx_len, ctx_cond, (zeros, zeros))
            lhs_x, _ = mixer(xs, seq, seq, 0, ends)
            if not last:
                cs = _tail(lhs_c, w_out, cs, *norm_args, bsz * ctx_len, ctx_cond)
            xs = _tail(lhs_x, w_out, xs, *norm_args, seq, 0)
        else:
            o = l // 2
            w_qkv = w_qkv_na[o].astype(BF16)
            n_heads = rpb_na.shape[1]
            scale = LOG2E * (d // n_heads) ** -0.5
            q_split, v_split = (0, d, scale), (2 * d, 3 * d, 1.0)
            w_k_t = _transposed_block(w_qkv_na, o, 1, d)
            q, v, kt = _inproj(xs, g_pre_mix[l], sh1, sc1, w_qkv, (q_split, v_split), (BF16, BF16),
                               seq, 0, wt=w_k_t)
            vc, kct = _inproj(cs, g_pre_mix[l], sh1, sc1, w_qkv, (v_split,), (BF16,),
                              bsz * ctx_len, ctx_cond, wt=w_k_t)
            grid = (bsz, seq // GRID_W, GRID_W, d)
            att = _natten(q.reshape(grid), kt, v.reshape(grid), kct, vc.reshape(bsz, ctx_len, d),
                          _natten_bias_table(rpb_na[o]))
            if not last:
                raise NotImplementedError("context update after an attention layer")
            xs = _tail([att.reshape(bsz * seq, d)], w_out_na[o].astype(BF16), xs, *norm_args, seq, 0)
    return xs.reshape(bsz, seq, d)
```
